```python
import jax
import jax.numpy as jnp
from jax import lax
import numpy as np

D_MODEL = 4096
BATCH = 2
SEQ = 8192
DEPTH = 4

N_META = 16
GRID_W = 64
MIX_W = D_MODEL
RW_HEAD = 64
RW_WIDTH = 3 * MIX_W // 8
RW_HEADS = RW_WIDTH // RW_HEAD
DECAY_LORA = max(32, int(round(1.8 * RW_WIDTH ** 0.5 / 32)) * 32)
ICL_LORA = max(32, int(round(1.8 * RW_WIDTH ** 0.5 / 32)) * 32)
GATE_LORA = max(32, int(round(0.6 * RW_WIDTH ** 0.8 / 32)) * 32)
RW_PROJ = 3 * RW_WIDTH + 2 * DECAY_LORA + 2 * ICL_LORA + GATE_LORA
GN_EPS = 64e-5
NA_HEAD = 128
NA_WIDTH = 3 * MIX_W // 8
NA_HEADS = NA_WIDTH // NA_HEAD
NA_KR = 8
NA_KC = 16
NA_PROJ = 3 * NA_WIDTH
GQ_HEAD = 128
GQ_WIDTH = MIX_W - RW_WIDTH - NA_WIDTH
GQ_HEADS = GQ_WIDTH // GQ_HEAD
GQ_KV_HEADS = 2
GQ_GROUP = GQ_HEADS // GQ_KV_HEADS
GQ_KV_W = GQ_KV_HEADS * GQ_HEAD
GQ_PROJ = GQ_WIDTH + 2 * GQ_KV_W
Q_BLOCK = 128
ROPE_THETA = 10000.0
ROPE_AXIS_DIM = GQ_HEAD // 2
QK_EPS = 1e-6
PROJ_W = RW_PROJ + NA_PROJ + GQ_PROJ
D_FF = 3 * D_MODEL // 2
LN_EPS = 1e-5
ALPHA = (2.0 * DEPTH) ** 0.25
BETA = (8.0 * DEPTH) ** -0.25

kernel_name = "hybrid_rwkv7_natten_gqa_deepnorm_encoder"


def layer_norm(x, g, b):
    xf = x.astype(jnp.float32)
    mu = jnp.mean(xf, -1, keepdims=True)
    var = jnp.mean(jnp.square(xf - mu), -1, keepdims=True)
    return ((xf - mu) * lax.rsqrt(var + LN_EPS) * g + b).astype(x.dtype)


def swiglu(x, w_in, w_out):
    gate, up = jnp.split(x @ w_in, 2, axis=-1)
    return (jax.nn.silu(gate) * up) @ w_out


def bi_token_shift(p, mu_prev, mu_next):
    prev = jnp.pad(p[:, :-1], ((0, 0), (1, 0), (0, 0)))
    nxt = jnp.pad(p[:, 1:], ((0, 0), (0, 1), (0, 0)))
    return p + mu_prev * (prev - p) + mu_next * (nxt - p)


def wkv7_scan(r, w, k, v, a_vec, b_vec, reverse):
    B, L, H, N = r.shape

    def step(S, inp):
        r_t, w_t, k_t, v_t, a_t, b_t = inp
        sa = jnp.einsum('bhvk,bhk->bhv', S, a_t)
        S = S * w_t[:, :, None, :] + sa[..., None] * b_t[:, :, None, :] + v_t[..., None] * k_t[:, :, None, :]
        return S, jnp.einsum('bhvk,bhk->bhv', S, r_t)

    xs = tuple(jnp.moveaxis(t, 1, 0) for t in (r, w, k, v, a_vec, b_vec))
    S0 = jnp.zeros((B, H, N, N), jnp.float32)
    _, out = lax.scan(step, S0, xs, reverse=reverse)
    return jnp.moveaxis(out, 0, 1)


def rwkv7_bidir(p, mu_prev, mu_next, w0, w_up, a0, a_up, g_up, k_k, k_a, r_k, gn_w, gn_b):
    B, L, _ = p.shape
    C = RW_WIDTH
    p = bi_token_shift(p, mu_prev, mu_next).astype(jnp.float32)
    r, k, v = p[..., :C], p[..., C:2 * C], p[..., 2 * C:3 * C]
    o = 3 * C
    w_dn = p[..., o:o + 2 * DECAY_LORA].reshape(B, L, 2, DECAY_LORA)
    o += 2 * DECAY_LORA
    a_dn = p[..., o:o + 2 * ICL_LORA].reshape(B, L, 2, ICL_LORA)
    o += 2 * ICL_LORA
    g_dn = p[..., o:]
    w_log = w0 + jnp.einsum('bldr,drc->bldc', jnp.tanh(w_dn), w_up)
    decay = jnp.exp(-jnp.exp(-jax.nn.softplus(-w_log) - 0.5))
    a = jax.nn.sigmoid(a0 + jnp.einsum('bldr,drc->bldc', a_dn, a_up))
    g = jax.nn.sigmoid(g_dn) @ g_up

    def heads(t):
        return t.reshape(t.shape[:-1] + (RW_HEADS, RW_HEAD))

    kk = heads(k * k_k)
    kk = kk / jnp.maximum(jnp.sqrt(jnp.sum(kk * kk, -1, keepdims=True)), 1e-12)
    a_h = heads(a)
    k_d = heads(k)[:, :, None] * (1.0 + (a_h - 1.0) * heads(k_a))
    r_h, v_h, dec_h = heads(r), heads(v), heads(decay)
    o_f = wkv7_scan(r_h, dec_h[:, :, 0], k_d[:, :, 0], v_h, -kk, kk * a_h[:, :, 0], reverse=False)
    o_b = wkv7_scan(r_h, dec_h[:, :, 1], k_d[:, :, 1], v_h, -kk, kk * a_h[:, :, 1], reverse=True)
    y = o_f + o_b
    mu = jnp.mean(y, -1, keepdims=True)
    var = jnp.mean(jnp.square(y - mu), -1, keepdims=True)
    y = ((y - mu) * lax.rsqrt(var + GN_EPS)).reshape(B, L, C) * gn_w + gn_b
    k_bonus = 0.5 * (k_d[:, :, 0] + k_d[:, :, 1])
    bonus = jnp.sum(r_h * k_bonus * r_k, -1, keepdims=True) * v_h
    return (y + bonus.reshape(B, L, C)) * g


def neighborhood_attention(p, rpb):
    B, L, _ = p.shape
    q, k, v = [t.reshape(B, L, NA_HEADS, NA_HEAD) for t in jnp.split(p, 3, axis=-1)]
    q = q * NA_HEAD ** -0.5
    qm, km, vm = q[:, :N_META], k[:, :N_META], v[:, :N_META]
    n_real = L - N_META
    rows = n_real // GRID_W
    kr = min(NA_KR, rows)

    def grid(t):
        return t[:, N_META:].reshape(B, rows, GRID_W, NA_HEADS, NA_HEAD)

    qg, kg, vg = grid(q), grid(k), grid(v)
    cols = np.arange(GRID_W)
    c0 = np.clip(cols - NA_KC // 2, 0, GRID_W - NA_KC)
    col_idx = c0[:, None] + np.arange(NA_KC)[None, :]
    col_off = col_idx - cols[:, None] + NA_KC - 1
    rpb_c = rpb[:, :, col_off]

    def row_block(i):
        r0 = jnp.clip(i - kr // 2, 0, rows - kr)
        q_i = lax.dynamic_index_in_dim(qg, i, axis=1, keepdims=False)
        k_rows = lax.dynamic_slice_in_dim(kg, r0, kr, axis=1)
        v_rows = lax.dynamic_slice_in_dim(vg, r0, kr, axis=1)
        k_win = k_rows[:, :, col_idx]
        v_win = v_rows[:, :, col_idx]
        row_off = r0 + jnp.arange(kr) - i + NA_KR - 1
        bias = jnp.take(rpb_c, row_off, axis=1).transpose(0, 2, 1, 3)
        s_win = jnp.einsum('bjhd,brjchd->bhjrc', q_i, k_win).astype(jnp.float32) + bias[None]
        s_meta = jnp.einsum('bjhd,bmhd->bhjm', q_i, km).astype(jnp.float32)
        s = jnp.concatenate([s_meta, s_win.reshape(B, NA_HEADS, GRID_W, kr * NA_KC)], axis=-1)
        pr = jax.nn.softmax(s, axis=-1).astype(v.dtype)
        p_meta = pr[..., :N_META]
        p_win = pr[..., N_META:].reshape(B, NA_HEADS, GRID_W, kr, NA_KC)
        return (jnp.einsum('bhjm,bmhd->bjhd', p_meta, vm)
                + jnp.einsum('bhjrc,brjchd->bjhd', p_win, v_win))

    out = lax.map(row_block, jnp.arange(rows))
    out = jnp.moveaxis(out, 0, 1).reshape(B, n_real, NA_WIDTH)
    s_m = jnp.einsum('bmhd,bnhd->bhmn', qm, km).astype(jnp.float32)
    p_m = jax.nn.softmax(s_m, axis=-1).astype(v.dtype)
    out_meta = jnp.einsum('bhmn,bnhd->bmhd', p_m, vm).reshape(B, N_META, NA_WIDTH)
    return jnp.concatenate([out_meta, out], axis=1)


def axial_rope_tables(L):
    n_real = L - N_META
    t = jnp.arange(n_real)
    row = (t // GRID_W).astype(jnp.float32)
    col = (t % GRID_W).astype(jnp.float32)
    half = ROPE_AXIS_DIM // 2
    inv = ROPE_THETA ** (-jnp.arange(half, dtype=jnp.float32) / half)
    ang = jnp.concatenate([row[:, None] * inv, col[:, None] * inv], axis=-1)
    ang = jnp.concatenate([jnp.zeros((N_META, ROPE_AXIS_DIM), jnp.float32), ang], axis=0)
    return jnp.cos(ang), jnp.sin(ang)


def rms_rope(x, gain, cos, sin):
    xf = x.astype(jnp.float32)
    xf = xf * lax.rsqrt(jnp.mean(xf * xf, -1, keepdims=True) + QK_EPS) * gain
    x1, x2 = xf[..., :GQ_HEAD // 2], xf[..., GQ_HEAD // 2:]
    c, s = cos[None, :, None, :], sin[None, :, None, :]
    return jnp.concatenate([x1 * c - x2 * s, x2 * c + x1 * s], axis=-1).astype(x.dtype)


def gqa_block_attend(q, k, v):
    s = jnp.einsum('bqkgd,bskd->bkgqs', q, k).astype(jnp.float32)
    pr = jax.nn.softmax(s, axis=-1).astype(v.dtype)
    return jnp.einsum('bkgqs,bskd->bqkgd', pr, v)


def gqa_axial(p, q_gain, k_gain):
    B, L, _ = p.shape
    q = p[..., :GQ_WIDTH].reshape(B, L, GQ_HEADS, GQ_HEAD)
    k = p[..., GQ_WIDTH:GQ_WIDTH + GQ_KV_W].reshape(B, L, GQ_KV_HEADS, GQ_HEAD)
    v = p[..., GQ_WIDTH + GQ_KV_W:].reshape(B, L, GQ_KV_HEADS, GQ_HEAD)
    cos, sin = axial_rope_tables(L)
    q = rms_rope(q, q_gain, cos, sin) * GQ_HEAD ** -0.5
    k = rms_rope(k, k_gain, cos, sin)
    q = q.reshape(B, L, GQ_KV_HEADS, GQ_GROUP, GQ_HEAD)
    out_meta = gqa_block_attend(q[:, :N_META], k, v).reshape(B, N_META, GQ_WIDTH)
    n_real = L - N_META
    n_blocks = n_real // Q_BLOCK
    qb = jnp.moveaxis(q[:, N_META:].reshape(B, n_blocks, Q_BLOCK, GQ_KV_HEADS, GQ_GROUP, GQ_HEAD), 1, 0)
    out = lax.map(lambda qq: gqa_block_attend(qq, k, v), qb)
    out = jnp.moveaxis(out, 0, 1).reshape(B, n_real, GQ_WIDTH)
    return jnp.concatenate([out_meta, out], axis=1)


def setup_inputs(seed: int = 0) -> dict:
    key = jax.random.key(seed)
    ks = jax.random.split(key, 32)
    f32 = jnp.float32

    def nrm(k, shape, s):
        return jax.random.normal(k, shape, f32) * s

    def uni(k, shape, lo, hi):
        return jax.random.uniform(k, shape, f32, lo, hi)

    C = RW_WIDTH
    return {
        "x": nrm(ks[0], (BATCH, SEQ, D_MODEL), 1.0),
        "meta_tokens": nrm(ks[1], (N_META, D_MODEL), 1.0),
        "ffn1_w_in": nrm(ks[2], (DEPTH, D_MODEL, 2 * D_FF), D_MODEL ** -0.5),
        "ffn1_w_out": nrm(ks[3], (DEPTH, D_FF, D_MODEL), BETA * D_FF ** -0.5),
        "ln_ffn1_g": 1.0 + nrm(ks[4], (DEPTH, D_MODEL), 0.01),
        "ln_ffn1_b": nrm(ks[5], (DEPTH, D_MODEL), 0.01),
        "w_in": nrm(ks[6], (DEPTH, D_MODEL, PROJ_W), D_MODEL ** -0.5),
        "rw_mu_prev": uni(ks[7], (DEPTH, RW_PROJ), 0.0, 0.5),
        "rw_mu_next": uni(ks[8], (DEPTH, RW_PROJ), 0.0, 0.5),
        "rw_w0": uni(ks[9], (DEPTH, 2, C), -6.0, 0.0),
        "rw_w_up": nrm(ks[10], (DEPTH, 2, DECAY_LORA, C), 0.5 * DECAY_LORA ** -0.5),
        "rw_a0": nrm(ks[11], (DEPTH, 2, C), 0.1),
        "rw_a_up": nrm(ks[12], (DEPTH, 2, ICL_LORA, C), ICL_LORA ** -0.5),
        "rw_g_up": nrm(ks[13], (DEPTH, GATE_LORA, C), GATE_LORA ** -0.5),
        "rw_k_k": 0.85 + nrm(ks[14], (DEPTH, C), 0.02),
        "rw_k_a": 1.0 + nrm(ks[15], (DEPTH, C), 0.02),
        "rw_r_k": nrm(ks[16], (DEPTH, RW_HEADS, RW_HEAD), 0.1),
        "rw_gn_w": 1.0 + nrm(ks[17], (DEPTH, C), 0.02),
        "rw_gn_b": nrm(ks[18], (DEPTH, C), 0.01),
        "na_rpb": nrm(ks[19], (DEPTH, NA_HEADS, 2 * NA_KR - 1, 2 * NA_KC - 1), 0.02),
        "gq_q_gain": 1.0 + nrm(ks[20], (DEPTH, GQ_HEAD), 0.02),
        "gq_k_gain": 1.0 + nrm(ks[21], (DEPTH, GQ_HEAD), 0.02),
        "w_out": nrm(ks[22], (DEPTH, MIX_W, D_MODEL), BETA * MIX_W ** -0.5),
        "ln_mix_g": 1.0 + nrm(ks[23], (DEPTH, D_MODEL), 0.01),
        "ln_mix_b": nrm(ks[24], (DEPTH, D_MODEL), 0.01),
        "ffn2_w_in": nrm(ks[25], (DEPTH, D_MODEL, 2 * D_FF), D_MODEL ** -0.5),
        "ffn2_w_out": nrm(ks[26], (DEPTH, D_FF, D_MODEL), BETA * D_FF ** -0.5),
        "ln_ffn2_g": 1.0 + nrm(ks[27], (DEPTH, D_MODEL), 0.01),
        "ln_ffn2_b": nrm(ks[28], (DEPTH, D_MODEL), 0.01),
    }


def reference(x, meta_tokens, ffn1_w_in, ffn1_w_out, ln_ffn1_g, ln_ffn1_b, w_in,
              rw_mu_prev, rw_mu_next, rw_w0, rw_w_up, rw_a0, rw_a_up, rw_g_up,
              rw_k_k, rw_k_a, rw_r_k, rw_gn_w, rw_gn_b, na_rpb, gq_q_gain, gq_k_gain,
              w_out, ln_mix_g, ln_mix_b, ffn2_w_in, ffn2_w_out, ln_ffn2_g, ln_ffn2_b):
    B = x.shape[0]
    meta = jnp.broadcast_to(meta_tokens[None].astype(x.dtype), (B, N_META, D_MODEL))
    h = jnp.concatenate([meta, x], axis=1)
    for l in range(DEPTH):
        h = layer_norm(ALPHA * h + 0.5 * swiglu(h, ffn1_w_in[l], ffn1_w_out[l]), ln_ffn1_g[l], ln_ffn1_b[l])
        p = h @ w_in[l]
        y_a = rwkv7_bidir(p[..., :RW_PROJ], rw_mu_prev[l], rw_mu_next[l], rw_w0[l], rw_w_up[l],
                          rw_a0[l], rw_a_up[l], rw_g_up[l], rw_k_k[l], rw_k_a[l], rw_r_k[l],
                          rw_gn_w[l], rw_gn_b[l]).astype(h.dtype)
        y_b = neighborhood_attention(p[..., RW_PROJ:RW_PROJ + NA_PROJ], na_rpb[l])
        y_c = gqa_axial(p[..., RW_PROJ + NA_PROJ:], gq_q_gain[l], gq_k_gain[l])
        mix = jnp.concatenate([y_a, y_b, y_c], axis=-1) @ w_out[l]
        h = layer_norm(ALPHA * h + mix, ln_mix_g[l], ln_mix_b[l])
        h = layer_norm(ALPHA * h + 0.5 * swiglu(h, ffn2_w_in[l], ffn2_w_out[l]), ln_ffn2_g[l], ln_ffn2_b[l])
    return h[:, N_META:]
```

```python
import functools
import math

import numpy as np
import jax
import jax.numpy as jnp
from jax import lax
from jax.experimental import pallas as pl
from jax.experimental.pallas import tpu as pltpu

F32 = jnp.float32
BF16 = jnp.bfloat16
HIGHEST = lax.Precision.HIGHEST

D_MODEL = 4096
DEPTH = 4
N_META = 16
GRID_W = 64
RW_HEAD = 64
RW_WIDTH = 1536
RW_HEADS = RW_WIDTH // RW_HEAD
RW_PAIRS = RW_HEADS // 2
DECAY_LORA = 64
ICL_LORA = 64
GATE_LORA = 224
RW_PROJ = 3 * RW_WIDTH + 2 * DECAY_LORA + 2 * ICL_LORA + GATE_LORA
GN_EPS = 64e-5
NA_HEAD = 128
NA_WIDTH = 1536
NA_HEADS = NA_WIDTH // NA_HEAD
NA_KR = 8
NA_KC = 16
NA_PROJ = 3 * NA_WIDTH
GQ_HEAD = 128
GQ_WIDTH = 1024
GQ_HEADS = GQ_WIDTH // GQ_HEAD
GQ_KV_HEADS = 2
GQ_GROUP = GQ_HEADS // GQ_KV_HEADS
GQ_KV_W = GQ_KV_HEADS * GQ_HEAD
GQ_PROJ = GQ_WIDTH + 2 * GQ_KV_W
ROPE_THETA = 10000.0
QK_EPS = 1e-6
D_FF = 3 * D_MODEL // 2
LN_EPS = 1e-5
ALPHA = (2.0 * DEPTH) ** 0.25

LANE = 128
META_BLK = 128
META_OFF = META_BLK - N_META
RW_PROJ_PAD = 5120
GATE_PAD = 256
CHUNK = 64
PREP_ROWS = 128
NEG = -1e30
VMEM_LIMIT = 56 * 2 ** 20


def _cparams(sem):
    return pltpu.CompilerParams(dimension_semantics=sem, vmem_limit_bytes=VMEM_LIMIT)


def _row_tile(total, cap):
    best = LANE
    for t in range(LANE, cap + 1, LANE):
        if total % t == 0:
            best = t
    return best


def _dot_nt(a, b, precision=None):
    return lax.dot_general(a, b, (((1,), (1,)), ((), ())), preferred_element_type=F32, precision=precision)


def _dot_tn(a, b, precision=None):
    return lax.dot_general(a, b, (((0,), (0,)), ((), ())), preferred_element_type=F32, precision=precision)


def _dot(a, b, precision=None):
    return jnp.dot(a, b, preferred_element_type=F32, precision=precision)


def _mm_kernel(x_ref, w_ref, o_ref):
    o_ref[...] = _dot(x_ref[...], w_ref[...]).astype(o_ref.dtype)


def _matmul(x, w, out_dtype, tn=512):
    T, K = x.shape
    N = w.shape[1]
    tm = _row_tile(T, 1280)
    return pl.pallas_call(
        _mm_kernel,
        grid=(T // tm, N // tn),
        in_specs=[pl.BlockSpec((tm, K), lambda i, j: (i, 0)),
                  pl.BlockSpec((K, tn), lambda i, j: (0, j))],
        out_specs=pl.BlockSpec((tm, tn), lambda i, j: (i, j)),
        out_shape=jax.ShapeDtypeStruct((T, N), out_dtype),
        compiler_params=_cparams(("parallel", "arbitrary")),
        name="proj_matmul",
    )(x, w)


def _swiglu_kernel(x_ref, wg_ref, wu_ref, o_ref):
    x = x_ref[...]
    g = _dot(x, wg_ref[...])
    u = _dot(x, wu_ref[...])
    o_ref[...] = (g * jax.nn.sigmoid(g) * u).astype(o_ref.dtype)


def _swiglu_matmul(x, w_in, tn=512):
    T, K = x.shape
    F = w_in.shape[1] // 2
    tm = _row_tile(T, 1280)
    nj = F // tn
    return pl.pallas_call(
        _swiglu_kernel,
        grid=(T // tm, nj),
        in_specs=[pl.BlockSpec((tm, K), lambda i, j: (i, 0)),
                  pl.BlockSpec((K, tn), lambda i, j: (0, j)),
                  pl.BlockSpec((K, tn), lambda i, j: (0, j + nj))],
        out_specs=pl.BlockSpec((tm, tn), lambda i, j: (i, j)),
        out_shape=jax.ShapeDtypeStruct((T, F), BF16),
        compiler_params=_cparams(("parallel", "arbitrary")),
        name="ffn_swiglu",
    )(x, w_in, w_in)


def _mm_resid_kernel(x_ref, w_ref, h_ref, o_ref, *, scale):
    o_ref[...] = ALPHA * h_ref[...] + scale * _dot(x_ref[...], w_ref[...])


def _matmul_resid(x, w, h, scale, tn=512):
    T, K = x.shape
    N = w.shape[1]
    tm = _row_tile(T, 640)
    return pl.pallas_call(
        functools.partial(_mm_resid_kernel, scale=scale),
        grid=(T // tm, N // tn),
        in_specs=[pl.BlockSpec((tm, K), lambda i, j: (i, 0)),
                  pl.BlockSpec((K, tn), lambda i, j: (0, j)),
                  pl.BlockSpec((tm, tn), lambda i, j: (i, j))],
        out_specs=pl.BlockSpec((tm, tn), lambda i, j: (i, j)),
        out_shape=jax.ShapeDtypeStruct((T, N), F32),
        compiler_params=_cparams(("parallel", "arbitrary")),
        name="resid_matmul",
    )(x, w, h)


def _ln_kernel(z_ref, g_ref, b_ref, h_ref, hb_ref):
    z = z_ref[...]
    mu = jnp.mean(z, axis=-1, keepdims=True)
    zc = z - mu
    var = jnp.mean(zc * zc, axis=-1, keepdims=True)
    y = zc * lax.rsqrt(var + LN_EPS) * g_ref[...] + b_ref[...]
    h_ref[...] = y
    hb_ref[...] = y.astype(BF16)


def _layer_norm(z, g, b):
    T, D = z.shape
    tm = _row_tile(T, 256)
    return pl.pallas_call(
        _ln_kernel,
        grid=(T // tm,),
        in_specs=[pl.BlockSpec((tm, D), lambda i: (i, 0)),
                  pl.BlockSpec((1, D), lambda i: (0, 0)),
                  pl.BlockSpec((1, D), lambda i: (0, 0))],
        out_specs=[pl.BlockSpec((tm, D), lambda i: (i, 0)),
                   pl.BlockSpec((tm, D), lambda i: (i, 0))],
        out_shape=[jax.ShapeDtypeStruct((T, D), F32), jax.ShapeDtypeStruct((T, D), BF16)],
        compiler_params=_cparams(("parallel",)),
        name="layer_norm",
    )(z, g.reshape(1, D), b.reshape(1, D))


def _ffn(h, hb, w_in, w_out, g, b):
    act = _swiglu_matmul(hb, w_in.astype(BF16))
    z = _matmul_resid(act, w_out.astype(BF16), h, 0.5)
    return _layer_norm(z, g, b)


def _rope_tables(B, S):
    t = jnp.arange(S)
    row = (t // GRID_W).astype(F32)
    col = (t % GRID_W).astype(F32)
    half = GQ_HEAD // 4
    inv = ROPE_THETA ** (-jnp.arange(half, dtype=F32) / half)
    ang = jnp.concatenate([row[:, None] * inv, col[:, None] * inv], axis=-1)
    cos, sin = jnp.cos(ang), jnp.sin(ang)
    cos2 = jnp.concatenate([cos, cos], axis=-1)
    sin2 = jnp.concatenate([-sin, sin], axis=-1)
    pad_c = jnp.ones((B * META_BLK, GQ_HEAD), F32)
    pad_s = jnp.zeros((B * META_BLK, GQ_HEAD), F32)
    return (jnp.concatenate([jnp.tile(cos2, (B, 1)), pad_c], axis=0),
            jnp.concatenate([jnp.tile(sin2, (B, 1)), pad_s], axis=0))


def _gq_prep_kernel(p_ref, cos_ref, sin_ref, qg_ref, kg_ref, q_ref, k_ref, v_ref):
    cos = cos_ref[...]
    sin = sin_ref[...]

    def rms_rope(x, gain):
        x = x * lax.rsqrt(jnp.mean(x * x, axis=-1, keepdims=True) + QK_EPS) * gain
        return x * cos + pltpu.roll(x, GQ_HEAD // 2, axis=1) * sin

    for hd in range(GQ_HEADS):
        sl = slice(hd * GQ_HEAD, (hd + 1) * GQ_HEAD)
        q_ref[:, sl] = (rms_rope(p_ref[:, sl], qg_ref[...]) * GQ_HEAD ** -0.5).astype(BF16)
    for hd in range(GQ_KV_HEADS):
        sl = slice(hd * GQ_HEAD, (hd + 1) * GQ_HEAD)
        slk = slice(GQ_WIDTH + hd * GQ_HEAD, GQ_WIDTH + (hd + 1) * GQ_HEAD)
        slv = slice(GQ_WIDTH + GQ_KV_W + hd * GQ_HEAD, GQ_WIDTH + GQ_KV_W + (hd + 1) * GQ_HEAD)
        k_ref[:, sl] = rms_rope(p_ref[:, slk], kg_ref[...]).astype(BF16)
        v_ref[:, sl] = p_ref[:, slv].astype(BF16)


def _gq_prep(p_gq, cos_t, sin_t, q_gain, k_gain):
    T = p_gq.shape[0]
    tm = _row_tile(T, 256)
    row = lambda i: (i, 0)
    fix = lambda i: (0, 0)
    return pl.pallas_call(
        _gq_prep_kernel,
        grid=(T // tm,),
        in_specs=[pl.BlockSpec((tm, GQ_PROJ), row),
                  pl.BlockSpec((tm, GQ_HEAD), row),
                  pl.BlockSpec((tm, GQ_HEAD), row),
                  pl.BlockSpec((1, GQ_HEAD), fix),
                  pl.BlockSpec((1, GQ_HEAD), fix)],
        out_specs=[pl.BlockSpec((tm, GQ_WIDTH), row),
                   pl.BlockSpec((tm, GQ_KV_W), row),
                   pl.BlockSpec((tm, GQ_KV_W), row)],
        out_shape=[jax.ShapeDtypeStruct((T, GQ_WIDTH), BF16),
                   jax.ShapeDtypeStruct((T, GQ_KV_W), BF16),
                   jax.ShapeDtypeStruct((T, GQ_KV_W), BF16)],
        compiler_params=_cparams(("parallel",)),
        name="gqa_prep",
    )(p_gq, cos_t, sin_t, q_gain.reshape(1, GQ_HEAD), k_gain.reshape(1, GQ_HEAD))


def _flash_kernel(q_ref, k_ref, v_ref, km_ref, vm_ref, o_ref, m_sc, l_sc, acc_sc, *, nk):
    ki = pl.program_id(3)
    tq = q_ref.shape[0]
    q = jnp.concatenate([q_ref[:, g * GQ_HEAD:(g + 1) * GQ_HEAD] for g in range(GQ_GROUP)], axis=0)

    def attend(k, v, mask):
        s = _dot_nt(q, k)
        if mask is not None:
            s = jnp.where(mask, s, NEG)
        m_prev = m_sc[...]
        m_new = jnp.maximum(m_prev, jnp.max(s, axis=-1, keepdims=True))
        corr = jnp.exp(m_prev - m_new)
        p = jnp.exp(s - m_new)
        l_sc[...] = corr * l_sc[...] + jnp.sum(p, axis=-1, keepdims=True)
        acc_sc[...] = corr * acc_sc[...] + _dot(p.astype(BF16), v)
        m_sc[...] = m_new

    @pl.when(ki == 0)
    def _():
        m_sc[...] = jnp.full(m_sc.shape, NEG, F32)
        l_sc[...] = jnp.zeros(l_sc.shape, F32)
        acc_sc[...] = jnp.zeros(acc_sc.shape, F32)
        is_meta = lax.broadcasted_iota(jnp.int32, (1, META_BLK), 1) >= META_OFF
        attend(km_ref[...], vm_ref[...], is_meta)

    attend(k_ref[...], v_ref[...], None)

    @pl.when(ki == nk - 1)
    def _():
        o = acc_sc[...] / l_sc[...]
        for g in range(GQ_GROUP):
            o_ref[:, g * GQ_HEAD:(g + 1) * GQ_HEAD] = o[g * tq:(g + 1) * tq].astype(o_ref.dtype)


def _gqa(q, k, v, B, S):
    T = q.shape[0]
    tq = 128
    tk = _row_tile(S, 2048)
    nq_real = S // tq
    nq = nq_real + META_BLK // tq
    nk = S // tk
    meta_q = B * S // tq
    meta_k = B * S // META_BLK

    def q_map(b, g, qi, ki):
        return (jnp.where(qi < nq_real, b * nq_real + qi, meta_q + b * (META_BLK // tq) + (qi - nq_real)), g)

    return pl.pallas_call(
        functools.partial(_flash_kernel, nk=nk),
        grid=(B, GQ_KV_HEADS, nq, nk),
        in_specs=[pl.BlockSpec((tq, GQ_GROUP * GQ_HEAD), q_map),
                  pl.BlockSpec((tk, GQ_HEAD), lambda b, g, qi, ki: (b * nk + ki, g)),
                  pl.BlockSpec((tk, GQ_HEAD), lambda b, g, qi, ki: (b * nk + ki, g)),
                  pl.BlockSpec((META_BLK, GQ_HEAD), lambda b, g, qi, ki: (meta_k + b, g)),
                  pl.BlockSpec((META_BLK, GQ_HEAD), lambda b, g, qi, ki: (meta_k + b, g))],
        out_specs=pl.BlockSpec((tq, GQ_GROUP * GQ_HEAD), q_map),
        out_shape=jax.ShapeDtypeStruct((T, GQ_WIDTH), BF16),
        scratch_shapes=[pltpu.VMEM((GQ_GROUP * tq, 1), F32),
                        pltpu.VMEM((GQ_GROUP * tq, 1), F32),
                        pltpu.VMEM((GQ_GROUP * tq, GQ_HEAD), F32)],
        compiler_params=_cparams(("parallel", "parallel", "parallel", "arbitrary")),
        name="gqa_flash",
    )(q, k, v, k, v)


NA_QROWS = 8
NA_KROWS = 16
NA_QN = NA_QROWS * GRID_W
NA_KN = NA_KROWS * GRID_W


def _na_bias_tables(rpb):
    qi = np.arange(NA_QROWS)[:, None, None, None]
    qj = np.arange(GRID_W)[None, :, None, None]
    kr = np.arange(NA_KROWS)[None, None, :, None]
    kc = np.arange(GRID_W)[None, None, None, :]
    c0 = np.clip(qj - NA_KC // 2, 0, GRID_W - NA_KC)
    valid_c = (kc >= c0) & (kc < c0 + NA_KC)
    col_off = kc - qj + NA_KC - 1
    tabs = []
    for i_rel, r0_rel in ((qi, np.maximum(qi - NA_KR // 2, 0)),
                          (qi + NA_KR // 2, qi),
                          (qi + NA_KR, np.minimum(qi + NA_KR // 2, NA_KR))):
        valid_r = (kr >= r0_rel) & (kr < r0_rel + NA_KR)
        row_off = kr - i_rel + NA_KR - 1
        valid = np.broadcast_to(valid_r & valid_c, (NA_QROWS, GRID_W, NA_KROWS, GRID_W))
        ro = np.broadcast_to(np.clip(row_off, 0, 2 * NA_KR - 2), valid.shape)
        co = np.broadcast_to(np.clip(col_off, 0, 2 * NA_KC - 2), valid.shape)
        tab = jnp.where(valid[None], rpb[:, ro, co], NEG)
        tabs.append(tab.reshape(NA_HEADS, NA_QN, NA_KN))
    return jnp.stack(tabs, axis=1).astype(F32)


def _na_kernel(q_ref, k_ref, v_ref, km_ref, vm_ref, bias_ref, o_ref, *, rows):
    nblk = rows // NA_QROWS
    scale = NA_HEAD ** -0.5
    km = km_ref[...]
    vm = vm_ref[...]
    is_meta = lax.broadcasted_iota(jnp.int32, (1, META_BLK), 1) >= META_OFF

    def body(blk, carry):
        q = q_ref[pl.ds(pl.multiple_of(blk * NA_QN, NA_QN), NA_QN), :]
        krow = jnp.clip(blk * NA_QROWS - NA_KR // 2, 0, rows - NA_KROWS)
        ks = pl.multiple_of(krow * GRID_W, 4 * GRID_W)
        kw = k_ref[pl.ds(ks, NA_KN), :]
        vw = v_ref[pl.ds(ks, NA_KN), :]
        variant = jnp.where(blk == 0, 0, jnp.where(blk == nblk - 1, 2, 1))
        s = _dot_nt(q, kw) * scale + bias_ref[0, variant]
        sm = jnp.where(is_meta, _dot_nt(q, km) * scale, NEG)
        m = jnp.maximum(jnp.max(s, axis=-1, keepdims=True), jnp.max(sm, axis=-1, keepdims=True))
        p = jnp.exp(s - m)
        pm = jnp.exp(sm - m)
        l = jnp.sum(p, axis=-1, keepdims=True) + jnp.sum(pm, axis=-1, keepdims=True)
        o = (_dot(p.astype(BF16), vw) + _dot(pm.astype(BF16), vm)) / l
        o_ref[pl.ds(pl.multiple_of(blk * NA_QN, NA_QN), NA_QN), :] = o.astype(o_ref.dtype)
        return carry

    lax.fori_loop(0, nblk, body, 0)


def _na_meta_kernel(y_hbm, q_ref, km_ref, vm_ref, o_ref):
    del y_hbm
    is_meta = lax.broadcasted_iota(jnp.int32, (1, META_BLK), 1) >= META_OFF
    s = jnp.where(is_meta, _dot_nt(q_ref[...], km_ref[...]) * NA_HEAD ** -0.5, NEG)
    p = jnp.exp(s - jnp.max(s, axis=-1, keepdims=True))
    o = _dot(p.astype(BF16), vm_ref[...]) / jnp.sum(p, axis=-1, keepdims=True)
    o_ref[...] = o.astype(o_ref.dtype)


def _neighborhood_attention(p_na, rpb, B, S):
    T = p_na.shape[0]
    rows = S // GRID_W
    assert rows % NA_QROWS == 0 and rows >= NA_KROWS
    bias = _na_bias_tables(rpb)
    meta_blk = B * S // META_BLK
    y = pl.pallas_call(
        functools.partial(_na_kernel, rows=rows),
        grid=(B, NA_HEADS),
        in_specs=[pl.BlockSpec((S, NA_HEAD), lambda b, h: (b, h)),
                  pl.BlockSpec((S, NA_HEAD), lambda b, h: (b, NA_HEADS + h)),
                  pl.BlockSpec((S, NA_HEAD), lambda b, h: (b, 2 * NA_HEADS + h)),
                  pl.BlockSpec((META_BLK, NA_HEAD), lambda b, h: (meta_blk + b, NA_HEADS + h)),
                  pl.BlockSpec((META_BLK, NA_HEAD), lambda b, h: (meta_blk + b, 2 * NA_HEADS + h)),
                  pl.BlockSpec((1, 3, NA_QN, NA_KN), lambda b, h: (h, 0, 0, 0))],
        out_specs=pl.BlockSpec((S, NA_HEAD), lambda b, h: (b, h)),
        out_shape=jax.ShapeDtypeStruct((T, NA_WIDTH), BF16),
        compiler_params=_cparams(("parallel", "arbitrary")),
        name="na_window",
    )(p_na, p_na, p_na, p_na, p_na, bias)
    return pl.pallas_call(
        _na_meta_kernel,
        grid=(B, NA_HEADS),
        in_specs=[pl.BlockSpec(memory_space=pl.ANY),
                  pl.BlockSpec((META_BLK, NA_HEAD), lambda b, h: (meta_blk + b, h)),
                  pl.BlockSpec((META_BLK, NA_HEAD), lambda b, h: (meta_blk + b, NA_HEADS + h)),
                  pl.BlockSpec((META_BLK, NA_HEAD), lambda b, h: (meta_blk + b, 2 * NA_HEADS + h))],
        out_specs=pl.BlockSpec((META_BLK, NA_HEAD), lambda b, h: (meta_blk + b, h)),
        out_shape=jax.ShapeDtypeStruct((T, NA_WIDTH), BF16),
        input_output_aliases={0: 0},
        compiler_params=_cparams(("parallel", "arbitrary")),
        name="na_meta",
    )(y, p_na, p_na, p_na)


def _pair_sum_matrix(value):
    r = np.arange(LANE)[:, None] // RW_HEAD
    c = np.arange(LANE)[None, :] // RW_HEAD
    return jnp.asarray(np.where(r == c, value, 0.0), F32)


def _rw_prep_kernel(x_ref, pb_ref, nb_ref, mup_ref, mun_ref, w0_ref, wup_ref, a0_ref, aup_ref, gup_ref,
                    kk_w_ref, ka_ref, ones_ref,
                    r_ref, v_ref, kk_ref, lw_ref, kd_ref, bd_ref, g_ref, *, n_real_tiles, tiles_per_batch):
    i = pl.program_id(0)
    tl = x_ref.shape[0]
    C = RW_WIDTH
    is_meta = i >= n_real_tiles
    last = (i % tiles_per_batch) == tiles_per_batch - 1
    x = x_ref[...]
    rid = lax.broadcasted_iota(jnp.int32, (tl, 1), 0)
    prev = jnp.where(rid == 0, pb_ref[7:8, :], pltpu.roll(x, 1, axis=0))
    nxt = jnp.where(rid == tl - 1, nb_ref[0:1, :], pltpu.roll(x, tl - 1, axis=0))
    no_prev_upto = jnp.where(is_meta, META_OFF, -1)
    no_next_from = jnp.where(jnp.logical_and(jnp.logical_not(is_meta), last), tl - 1, tl)
    prev = jnp.where(rid <= no_prev_upto, 0.0, prev)
    nxt = jnp.where(rid >= no_next_from, 0.0, nxt)
    xs = x + mup_ref[...] * (prev - x) + mun_ref[...] * (nxt - x)

    live = rid >= jnp.where(is_meta, META_OFF, 0)
    r = xs[:, 0:C]
    k = xs[:, C:2 * C]
    v = xs[:, 2 * C:3 * C]
    o = 3 * C
    w_dn = xs[:, o:o + 2 * DECAY_LORA]
    a_dn = xs[:, o + 2 * DECAY_LORA:o + 2 * DECAY_LORA + 2 * ICL_LORA]
    g_dn = xs[:, o + 2 * DECAY_LORA + 2 * ICL_LORA:]
    w_log = w0_ref[...] + _dot(jnp.tanh(w_dn), wup_ref[...], HIGHEST)
    a = jax.nn.sigmoid(a0_ref[...] + _dot(a_dn, aup_ref[...], HIGHEST))
    g_ref[...] = _dot(jax.nn.sigmoid(g_dn), gup_ref[...], HIGHEST)
    logw = -math.exp(-0.5) * jax.nn.sigmoid(w_log)
    kk_raw = k * kk_w_ref[...]
    ka = ka_ref[...]
    ones = ones_ref[...]
    for hp in range(RW_PAIRS):
        sl = slice(hp * LANE, (hp + 1) * LANE)
        kr = kk_raw[:, sl]
        nrm = jnp.sqrt(_dot(kr * kr, ones, HIGHEST))
        kk = jnp.where(live, kr / jnp.maximum(nrm, 1e-12), 0.0)
        r_ref[hp] = jnp.where(live, r[:, sl], 0.0)
        v_ref[hp] = jnp.where(live, v[:, sl], 0.0)
        kk_ref[hp] = kk
        kp = k[:, sl]
        for d in range(2):
            sld = slice(d * C + hp * LANE, d * C + (hp + 1) * LANE)
            ad = a[:, sld]
            lw_ref[d, hp] = jnp.where(live, logw[:, sld], 0.0)
            kd_ref[d, hp] = jnp.where(live, kp * (1.0 + (ad - 1.0) * ka[:, sl]), 0.0)
            bd_ref[d, hp] = kk * ad


def _rw_prep(p_rw, prm, B, S):
    T = p_rw.shape[0]
    tl = PREP_ROWS
    n_real_tiles = B * S // tl
    tiles_per_batch = S // tl
    sub = tl // 8

    def prev_map(i):
        b = i // tiles_per_batch
        meta_last = (B * S + META_BLK * b + META_BLK - 8) // 8
        real = jnp.where(i % tiles_per_batch == 0, meta_last, i * sub - 1)
        return (jnp.where(i < n_real_tiles, real, jnp.maximum(i * sub - 1, 0)), 0)

    def next_map(i):
        real = jnp.where(i % tiles_per_batch == tiles_per_batch - 1, 0, (i + 1) * sub)
        return (jnp.where(i < n_real_tiles, real, (i - n_real_tiles) * (S // 8)), 0)

    fix = lambda i: (0, 0)
    pair = pl.BlockSpec((RW_PAIRS, tl, LANE), lambda i: (0, i, 0))
    pair2 = pl.BlockSpec((2, RW_PAIRS, tl, LANE), lambda i: (0, 0, i, 0))
    pair_shape = jax.ShapeDtypeStruct((RW_PAIRS, T, LANE), F32)
    pair2_shape = jax.ShapeDtypeStruct((2, RW_PAIRS, T, LANE), F32)
    C = RW_WIDTH
    return pl.pallas_call(
        functools.partial(_rw_prep_kernel, n_real_tiles=n_real_tiles, tiles_per_batch=tiles_per_batch),
        grid=(T // tl,),
        in_specs=[pl.BlockSpec((tl, RW_PROJ_PAD), lambda i: (i, 0)),
                  pl.BlockSpec((8, RW_PROJ_PAD), prev_map),
                  pl.BlockSpec((8, RW_PROJ_PAD), next_map),
                  pl.BlockSpec((1, RW_PROJ_PAD), fix),
                  pl.BlockSpec((1, RW_PROJ_PAD), fix),
                  pl.BlockSpec((1, 2 * C), fix),
                  pl.BlockSpec((2 * DECAY_LORA, 2 * C), fix),
                  pl.BlockSpec((1, 2 * C), fix),
                  pl.BlockSpec((2 * ICL_LORA, 2 * C), fix),
                  pl.BlockSpec((GATE_PAD, C), fix),
                  pl.BlockSpec((1, C), fix),
                  pl.BlockSpec((1, C), fix),
                  pl.BlockSpec((LANE, LANE), fix)],
        out_specs=[pair, pair, pair, pair2, pair2, pair2, pl.BlockSpec((tl, C), lambda i: (i, 0))],
        out_shape=[pair_shape, pair_shape, pair_shape, pair2_shape, pair2_shape, pair2_shape,
                   jax.ShapeDtypeStruct((T, C), F32)],
        compiler_params=_cparams(("parallel",)),
        name="rwkv_prep",
    )(p_rw, p_rw, p_rw, prm["mu_prev"], prm["mu_next"], prm["w0"], prm["w_up"], prm["a0"], prm["a_up"],
      prm["g_up"], prm["k_k"], prm["k_a"], _pair_sum_matrix(1.0))


SCAN_PAIRS_PER_ITER = 4


def _rw_scan_kernel(r_ref, v_ref, kk_ref, lw_ref, kd_ref, bd_ref, o_ref, s_sc):
    d = pl.program_id(0) % 2
    c = pl.program_id(1)

    @pl.when(c == 0)
    def _():
        s_sc[...] = jnp.zeros(s_sc.shape, F32)

    sign = 1 - 2 * d
    row = lax.broadcasted_iota(jnp.int32, (CHUNK, CHUNK), 0)
    col = lax.broadcasted_iota(jnp.int32, (CHUNK, CHUNK), 1)
    lag = (row - col) * sign
    incl = lag >= 0
    strict = lag > 0
    tri = jnp.where(incl, 1.0, 0.0)
    eye = jnp.where(row == col, 1.0, 0.0)
    row2 = lax.broadcasted_iota(jnp.int32, (CHUNK, 2 * CHUNK), 0)
    col2 = lax.broadcasted_iota(jnp.int32, (CHUNK, 2 * CHUNK), 1) % CHUNK
    incl2 = (row2 - col2) * sign >= 0
    n_levels = int(math.log2(CHUNK)) - 1

    def pair_step(hp):
        lw = lw_ref[0, hp]
        cum = _dot(tri, lw, HIGHEST)
        p_tot = jnp.exp(jnp.sum(lw, axis=0, keepdims=True))
        kk = kk_ref[hp]
        a_t = -kk * jnp.exp(cum - lw)
        r_t = r_ref[hp] * jnp.exp(cum)
        p_inv = jnp.exp(-cum)
        b_t = bd_ref[0, hp] * p_inv
        k_t = kd_ref[0, hp] * p_inv
        v = v_ref[hp]
        outs = []
        for e in range(2):
            sl = slice(e * RW_HEAD, (e + 1) * RW_HEAD)
            x = jnp.concatenate([a_t[:, sl], r_t[:, sl]], axis=0)
            y = jnp.concatenate([b_t[:, sl], k_t[:, sl]], axis=0)
            gram = _dot_nt(x, y, HIGHEST)
            a_ab = jnp.where(strict, gram[:CHUNK, :CHUNK], 0.0)
            a_ak = jnp.where(strict, gram[:CHUNK, CHUNK:], 0.0)
            a_r = jnp.where(incl2, gram[CHUNK:, :], 0.0)
            t_inv = eye + a_ab
            pw = a_ab
            for _ in range(n_levels):
                pw = _dot(pw, pw, HIGHEST)
                t_inv = t_inv + _dot(t_inv, pw, HIGHEST)
            s_old = s_sc[2 * hp + e]
            xs = _dot_nt(x, s_old, HIGHEST)
            vh = v[:, sl]
            u = _dot(t_inv, xs[:CHUNK] + _dot(a_ak, vh, HIGHEST), HIGHEST)
            uv = jnp.concatenate([u, vh], axis=0)
            outs.append(xs[CHUNK:] + _dot(a_r, uv, HIGHEST))
            s_sc[2 * hp + e] = (s_old + _dot_tn(uv, y, HIGHEST)) * p_tot[:, sl]
        o_ref[0, hp] = jnp.concatenate(outs, axis=1)

    def body(it, carry):
        for j in range(SCAN_PAIRS_PER_ITER):
            pair_step(it * SCAN_PAIRS_PER_ITER + j)
        return carry

    lax.fori_loop(0, RW_PAIRS // SCAN_PAIRS_PER_ITER, body, 0)


def _rw_scan(r, v, kk, lw, kd, bd, B, S):
    T = r.shape[1]
    nc = S // CHUNK + 1
    meta_chunk = (B * S + META_BLK - CHUNK) // CHUNK

    def blk(g, c):
        b = g // 2
        j = jnp.where(g % 2 == 0, c, nc - 1 - c)
        return jnp.where(j == 0, meta_chunk + b * (META_BLK // CHUNK), b * (S // CHUNK) + j - 1)

    shared = pl.BlockSpec((RW_PAIRS, CHUNK, LANE), lambda g, c: (0, blk(g, c), 0))
    per_dir = pl.BlockSpec((1, RW_PAIRS, CHUNK, LANE), lambda g, c: (g % 2, 0, blk(g, c), 0))
    return pl.pallas_call(
        _rw_scan_kernel,
        grid=(2 * B, nc),
        in_specs=[shared, shared, shared, per_dir, per_dir, per_dir],
        out_specs=per_dir,
        out_shape=jax.ShapeDtypeStruct((2, RW_PAIRS, T, LANE), F32),
        scratch_shapes=[pltpu.VMEM((RW_HEADS, RW_HEAD, RW_HEAD), F32)],
        compiler_params=_cparams(("parallel", "arbitrary")),
        name="rwkv_scan",
    )(r, v, kk, lw, kd, bd)


def _rw_post_kernel(o_ref, r_ref, v_ref, kd_ref, g_ref, rk_ref, gw_ref, gb_ref, mean_ref, ones_ref, y_ref):
    mean_m = mean_ref[...]
    ones = ones_ref[...]
    for hp in range(RW_PAIRS):
        sl = slice(hp * LANE, (hp + 1) * LANE)
        y = o_ref[0, hp] + o_ref[1, hp]
        yc = y - _dot(y, mean_m, HIGHEST)
        var = _dot(yc * yc, mean_m, HIGHEST)
        yn = yc * lax.rsqrt(var + GN_EPS) * gw_ref[:, sl] + gb_ref[:, sl]
        k_bonus = 0.5 * (kd_ref[0, hp] + kd_ref[1, hp])
        bonus = _dot(r_ref[hp] * k_bonus * rk_ref[:, sl], ones, HIGHEST) * v_ref[hp]
        y_ref[:, sl] = ((yn + bonus) * g_ref[:, sl]).astype(y_ref.dtype)


def _rw_post(o, r, v, kd, g, prm):
    T = g.shape[0]
    tl = _row_tile(T, 256)
    C = RW_WIDTH
    fix = lambda i: (0, 0)
    pair = pl.BlockSpec((RW_PAIRS, tl, LANE), lambda i: (0, i, 0))
    pair2 = pl.BlockSpec((2, RW_PAIRS, tl, LANE), lambda i: (0, 0, i, 0))
    return pl.pallas_call(
        _rw_post_kernel,
        grid=(T // tl,),
        in_specs=[pair2, pair, pair, pair2,
                  pl.BlockSpec((tl, C), lambda i: (i, 0)),
                  pl.BlockSpec((1, C), fix), pl.BlockSpec((1, C), fix), pl.BlockSpec((1, C), fix),
                  pl.BlockSpec((LANE, LANE), fix), pl.BlockSpec((LANE, LANE), fix)],
        out_specs=pl.BlockSpec((tl, C), lambda i: (i, 0)),
        out_shape=jax.ShapeDtypeStruct((T, C), BF16),
        compiler_params=_cparams(("parallel",)),
        name="rwkv_post",
    )(o, r, v, kd, g, prm["r_k"], prm["gn_w"], prm["gn_b"],
      _pair_sum_matrix(1.0 / RW_HEAD), _pair_sum_matrix(1.0))


def _block_diag2(m):
    z = jnp.zeros_like(m[0])
    return jnp.concatenate([jnp.concatenate([m[0], z], axis=1), jnp.concatenate([z, m[1]], axis=1)], axis=0)


def _rwkv7(p_rw, mu_prev, mu_next, w0, w_up, a0, a_up, g_up, k_k, k_a, r_k, gn_w, gn_b, B, S):
    C = RW_WIDTH
    pad = RW_PROJ_PAD - RW_PROJ
    prm = {
        "mu_prev": jnp.pad(mu_prev, (0, pad)).reshape(1, RW_PROJ_PAD),
        "mu_next": jnp.pad(mu_next, (0, pad)).reshape(1, RW_PROJ_PAD),
        "w0": w0.reshape(1, 2 * C),
        "w_up": _block_diag2(w_up),
        "a0": a0.reshape(1, 2 * C),
        "a_up": _block_diag2(a_up),
        "g_up": jnp.pad(g_up, ((0, GATE_PAD - GATE_LORA), (0, 0))),
        "k_k": k_k.reshape(1, C),
        "k_a": k_a.reshape(1, C),
        "r_k": r_k.reshape(1, C),
        "gn_w": gn_w.reshape(1, C),
        "gn_b": gn_b.reshape(1, C),
    }
    r, v, kk, lw, kd, bd, g = _rw_prep(p_rw, prm, B, S)
    o = _rw_scan(r, v, kk, lw, kd, bd, B, S)
    return _rw_post(o, r, v, kd, g, prm)


def kernel(x, meta_tokens, ffn1_w_in, ffn1_w_out, ln_ffn1_g, ln_ffn1_b, w_in, rw_mu_prev, rw_mu_next, rw_w0, rw_w_up, rw_a0, rw_a_up, rw_g_up, rw_k_k, rw_k_a, rw_r_k, rw_gn_w, rw_gn_b, na_rpb, gq_q_gain, gq_k_gain, w_out, ln_mix_g, ln_mix_b, ffn2_w_in, ffn2_w_out, ln_ffn2_g, ln_ffn2_b):
    B, S, D = x.shape
    assert D == D_MODEL and S % 1024 == 0
    meta = jnp.zeros((B, META_BLK, D), x.dtype).at[:, META_OFF:].set(
        jnp.broadcast_to(meta_tokens[None].astype(x.dtype), (B, N_META, D)))
    h = jnp.concatenate([x.reshape(B * S, D), meta.reshape(B * META_BLK, D)], axis=0)
    hb = h.astype(BF16)
    cos_t, sin_t = _rope_tables(B, S)
    for l in range(DEPTH):
        h, hb = _ffn(h, hb, ffn1_w_in[l], ffn1_w_out[l], ln_ffn1_g[l], ln_ffn1_b[l])
        w = w_in[l]
        w_rw = jnp.pad(w[:, :RW_PROJ], ((0, 0), (0, RW_PROJ_PAD - RW_PROJ))).astype(BF16)
        w_na = w[:, RW_PROJ:RW_PROJ + NA_PROJ].astype(BF16)
        w_gq = w[:, RW_PROJ + NA_PROJ:].astype(BF16)
        p_rw = _matmul(hb, w_rw, F32)
        p_na = _matmul(hb, w_na, BF16)
        p_gq = _matmul(hb, w_gq, F32)
        y_a = _rwkv7(p_rw, rw_mu_prev[l], rw_mu_next[l], rw_w0[l], rw_w_up[l], rw_a0[l], rw_a_up[l],
                     rw_g_up[l], rw_k_k[l], rw_k_a[l], rw_r_k[l], rw_gn_w[l], rw_gn_b[l], B, S)
        y_b = _neighborhood_attention(p_na, na_rpb[l], B, S)
        q, k, v = _gq_prep(p_gq, cos_t, sin_t, gq_q_gain[l], gq_k_gain[l])
        y_c = _gqa(q, k, v, B, S)
        y = jnp.concatenate([y_a, y_b, y_c], axis=-1)
        z = _matmul_resid(y, w_out[l].astype(BF16), h, 1.0)
        h, hb = _layer_norm(z, ln_mix_g[l], ln_mix_b[l])
        h, hb = _ffn(h, hb, ffn2_w_in[l], ffn2_w_out[l], ln_ffn2_g[l], ln_ffn2_b[l])
    return h[:B * S].reshape(B, S, D)
```

```python
import functools
import math

import numpy as np
import jax
import jax.numpy as jnp
from jax import lax
from jax.experimental import pallas as pl
from jax.experimental.pallas import tpu as pltpu

F32 = jnp.float32
BF16 = jnp.bfloat16
HIGHEST = lax.Precision.HIGHEST

D_MODEL = 4096
DEPTH = 4
N_META = 16
GRID_W = 64
RW_HEAD = 64
RW_WIDTH = 1536
RW_HEADS = RW_WIDTH // RW_HEAD
RW_PAIRS = RW_HEADS // 2
DECAY_LORA = 64
ICL_LORA = 64
GATE_LORA = 224
RW_PROJ = 3 * RW_WIDTH + 2 * DECAY_LORA + 2 * ICL_LORA + GATE_LORA
GN_EPS = 64e-5
NA_HEAD = 128
NA_WIDTH = 1536
NA_HEADS = NA_WIDTH // NA_HEAD
NA_KR = 8
NA_KC = 16
NA_PROJ = 3 * NA_WIDTH
GQ_HEAD = 128
GQ_WIDTH = 1024
GQ_HEADS = GQ_WIDTH // GQ_HEAD
GQ_KV_HEADS = 2
GQ_GROUP = GQ_HEADS // GQ_KV_HEADS
GQ_KV_W = GQ_KV_HEADS * GQ_HEAD
GQ_PROJ = GQ_WIDTH + 2 * GQ_KV_W
ROPE_THETA = 10000.0
QK_EPS = 1e-6
D_FF = 3 * D_MODEL // 2
LN_EPS = 1e-5
ALPHA = (2.0 * DEPTH) ** 0.25

LANE = 128
META_BLK = 128
META_OFF = META_BLK - N_META
RW_PROJ_PAD = 5120
GATE_PAD = 256
CHUNK = 64
PREP_ROWS = 128
NEG = -1e30
VMEM_LIMIT = 56 * 2 ** 20


def _cparams(sem):
    return pltpu.CompilerParams(dimension_semantics=sem, vmem_limit_bytes=VMEM_LIMIT)


def _row_tile(total, cap):
    best = LANE
    for t in range(LANE, cap + 1, LANE):
        if total % t == 0:
            best = t
    return best


def _dot_nt(a, b, precision=None):
    return lax.dot_general(a, b, (((1,), (1,)), ((), ())), preferred_element_type=F32, precision=precision)


def _dot_tn(a, b, precision=None):
    return lax.dot_general(a, b, (((0,), (0,)), ((), ())), preferred_element_type=F32, precision=precision)


def _dot(a, b, precision=None):
    return jnp.dot(a, b, preferred_element_type=F32, precision=precision)


def _mm_kernel(x_ref, w_ref, o_ref):
    o_ref[...] = _dot(x_ref[...], w_ref[...]).astype(o_ref.dtype)


def _matmul(x, w, out_dtype, tn=512):
    T, K = x.shape
    N = w.shape[1]
    tm = _row_tile(T, 1280)
    return pl.pallas_call(
        _mm_kernel,
        grid=(T // tm, N // tn),
        in_specs=[pl.BlockSpec((tm, K), lambda i, j: (i, 0)),
                  pl.BlockSpec((K, tn), lambda i, j: (0, j))],
        out_specs=pl.BlockSpec((tm, tn), lambda i, j: (i, j)),
        out_shape=jax.ShapeDtypeStruct((T, N), out_dtype),
        compiler_params=_cparams(("parallel", "arbitrary")),
        name="proj_matmul",
    )(x, w)


def _swiglu_kernel(x_ref, wg_ref, wu_ref, o_ref):
    x = x_ref[...]
    g = _dot(x, wg_ref[...])
    u = _dot(x, wu_ref[...])
    o_ref[...] = (g * jax.nn.sigmoid(g) * u).astype(o_ref.dtype)


def _swiglu_matmul(x, w_in, tn=512):
    T, K = x.shape
    F = w_in.shape[1] // 2
    tm = _row_tile(T, 1280)
    nj = F // tn
    return pl.pallas_call(
        _swiglu_kernel,
        grid=(T // tm, nj),
        in_specs=[pl.BlockSpec((tm, K), lambda i, j: (i, 0)),
                  pl.BlockSpec((K, tn), lambda i, j: (0, j)),
                  pl.BlockSpec((K, tn), lambda i, j: (0, j + nj))],
        out_specs=pl.BlockSpec((tm, tn), lambda i, j: (i, j)),
        out_shape=jax.ShapeDtypeStruct((T, F), BF16),
        compiler_params=_cparams(("parallel", "arbitrary")),
        name="ffn_swiglu",
    )(x, w_in, w_in)


def _mm_resid_kernel(x_ref, w_ref, h_ref, o_ref, *, scale):
    o_ref[...] = ALPHA * h_ref[...] + scale * _dot(x_ref[...], w_ref[...])


def _matmul_resid(x, w, h, scale, tn=512):
    T, K = x.shape
    N = w.shape[1]
    tm = _row_tile(T, 640)
    return pl.pallas_call(
        functools.partial(_mm_resid_kernel, scale=scale),
        grid=(T // tm, N // tn),
        in_specs=[pl.BlockSpec((tm, K), lambda i, j: (i, 0)),
                  pl.BlockSpec((K, tn), lambda i, j: (0, j)),
                  pl.BlockSpec((tm, tn), lambda i, j: (i, j))],
        out_specs=pl.BlockSpec((tm, tn), lambda i, j: (i, j)),
        out_shape=jax.ShapeDtypeStruct((T, N), F32),
        compiler_params=_cparams(("parallel", "arbitrary")),
        name="resid_matmul",
    )(x, w, h)


def _ln_kernel(z_ref, g_ref, b_ref, h_ref, hb_ref):
    z = z_ref[...]
    mu = jnp.mean(z, axis=-1, keepdims=True)
    zc = z - mu
    var = jnp.mean(zc * zc, axis=-1, keepdims=True)
    y = zc * lax.rsqrt(var + LN_EPS) * g_ref[...] + b_ref[...]
    h_ref[...] = y
    hb_ref[...] = y.astype(BF16)


def _layer_norm(z, g, b):
    T, D = z.shape
    tm = _row_tile(T, 256)
    return pl.pallas_call(
        _ln_kernel,
        grid=(T // tm,),
        in_specs=[pl.BlockSpec((tm, D), lambda i: (i, 0)),
                  pl.BlockSpec((1, D), lambda i: (0, 0)),
                  pl.BlockSpec((1, D), lambda i: (0, 0))],
        out_specs=[pl.BlockSpec((tm, D), lambda i: (i, 0)),
                   pl.BlockSpec((tm, D), lambda i: (i, 0))],
        out_shape=[jax.ShapeDtypeStruct((T, D), F32), jax.ShapeDtypeStruct((T, D), BF16)],
        compiler_params=_cparams(("parallel",)),
        name="layer_norm",
    )(z, g.reshape(1, D), b.reshape(1, D))


def _ffn(h, hb, w_in, w_out, g, b):
    act = _swiglu_matmul(hb, w_in.astype(BF16))
    z = _matmul_resid(act, w_out.astype(BF16), h, 0.5)
    return _layer_norm(z, g, b)


def _rope_tables(B, S):
    t = jnp.arange(S)
    row = (t // GRID_W).astype(F32)
    col = (t % GRID_W).astype(F32)
    half = GQ_HEAD // 4
    inv = ROPE_THETA ** (-jnp.arange(half, dtype=F32) / half)
    ang = jnp.concatenate([row[:, None] * inv, col[:, None] * inv], axis=-1)
    cos, sin = jnp.cos(ang), jnp.sin(ang)
    cos2 = jnp.concatenate([cos, cos], axis=-1)
    sin2 = jnp.concatenate([-sin, sin], axis=-1)
    pad_c = jnp.ones((B * META_BLK, GQ_HEAD), F32)
    pad_s = jnp.zeros((B * META_BLK, GQ_HEAD), F32)
    return (jnp.concatenate([jnp.tile(cos2, (B, 1)), pad_c], axis=0),
            jnp.concatenate([jnp.tile(sin2, (B, 1)), pad_s], axis=0))


def _gq_prep_kernel(p_ref, cos_ref, sin_ref, qg_ref, kg_ref, q_ref, k_ref, v_ref):
    cos = cos_ref[...]
    sin = sin_ref[...]

    def rms_rope(x, gain):
        x = x * lax.rsqrt(jnp.mean(x * x, axis=-1, keepdims=True) + QK_EPS) * gain
        return x * cos + pltpu.roll(x, GQ_HEAD // 2, axis=1) * sin

    for hd in range(GQ_HEADS):
        sl = slice(hd * GQ_HEAD, (hd + 1) * GQ_HEAD)
        q_ref[:, sl] = (rms_rope(p_ref[:, sl], qg_ref[...]) * GQ_HEAD ** -0.5).astype(BF16)
    for hd in range(GQ_KV_HEADS):
        sl = slice(hd * GQ_HEAD, (hd + 1) * GQ_HEAD)
        slk = slice(GQ_WIDTH + hd * GQ_HEAD, GQ_WIDTH + (hd + 1) * GQ_HEAD)
        slv = slice(GQ_WIDTH + GQ_KV_W + hd * GQ_HEAD, GQ_WIDTH + GQ_KV_W + (hd + 1) * GQ_HEAD)
        k_ref[:, sl] = rms_rope(p_ref[:, slk], kg_ref[...]).astype(BF16)
        v_ref[:, sl] = p_ref[:, slv].astype(BF16)


def _gq_prep(p_gq, cos_t, sin_t, q_gain, k_gain):
    T = p_gq.shape[0]
    tm = _row_tile(T, 256)
    row = lambda i: (i, 0)
    fix = lambda i: (0, 0)
    return pl.pallas_call(
        _gq_prep_kernel,
        grid=(T // tm,),
        in_specs=[pl.BlockSpec((tm, GQ_PROJ), row),
                  pl.BlockSpec((tm, GQ_HEAD), row),
                  pl.BlockSpec((tm, GQ_HEAD), row),
                  pl.BlockSpec((1, GQ_HEAD), fix),
                  pl.BlockSpec((1, GQ_HEAD), fix)],
        out_specs=[pl.BlockSpec((tm, GQ_WIDTH), row),
                   pl.BlockSpec((tm, GQ_KV_W), row),
                   pl.BlockSpec((tm, GQ_KV_W), row)],
        out_shape=[jax.ShapeDtypeStruct((T, GQ_WIDTH), BF16),
                   jax.ShapeDtypeStruct((T, GQ_KV_W), BF16),
                   jax.ShapeDtypeStruct((T, GQ_KV_W), BF16)],
        compiler_params=_cparams(("parallel",)),
        name="gqa_prep",
    )(p_gq, cos_t, sin_t, q_gain.reshape(1, GQ_HEAD), k_gain.reshape(1, GQ_HEAD))


def _flash_kernel(q_ref, k_ref, v_ref, km_ref, vm_ref, o_ref, m_sc, l_sc, acc_sc, *, nk):
    ki = pl.program_id(3)
    tq = q_ref.shape[0]
    q = jnp.concatenate([q_ref[:, g * GQ_HEAD:(g + 1) * GQ_HEAD] for g in range(GQ_GROUP)], axis=0)

    def attend(k, v, mask):
        s = _dot_nt(q, k)
        if mask is not None:
            s = jnp.where(mask, s, NEG)
        m_prev = m_sc[...]
        m_new = jnp.maximum(m_prev, jnp.max(s, axis=-1, keepdims=True))
        corr = jnp.exp(m_prev - m_new)
        p = jnp.exp(s - m_new)
        l_sc[...] = corr * l_sc[...] + jnp.sum(p, axis=-1, keepdims=True)
        acc_sc[...] = corr * acc_sc[...] + _dot(p.astype(BF16), v)
        m_sc[...] = m_new

    @pl.when(ki == 0)
    def _():
        m_sc[...] = jnp.full(m_sc.shape, NEG, F32)
        l_sc[...] = jnp.zeros(l_sc.shape, F32)
        acc_sc[...] = jnp.zeros(acc_sc.shape, F32)
        is_meta = lax.broadcasted_iota(jnp.int32, (1, META_BLK), 1) >= META_OFF
        attend(km_ref[...], vm_ref[...], is_meta)

    attend(k_ref[...], v_ref[...], None)

    @pl.when(ki == nk - 1)
    def _():
        o = acc_sc[...] / l_sc[...]
        for g in range(GQ_GROUP):
            o_ref[:, g * GQ_HEAD:(g + 1) * GQ_HEAD] = o[g * tq:(g + 1) * tq].astype(o_ref.dtype)


def _gqa(q, k, v, B, S):
    T = q.shape[0]
    tq = 128
    tk = _row_tile(S, 2048)
    nq_real = S // tq
    nq = nq_real + META_BLK // tq
    nk = S // tk
    meta_q = B * S // tq
    meta_k = B * S // META_BLK

    def q_map(b, g, qi, ki):
        return (jnp.where(qi < nq_real, b * nq_real + qi, meta_q + b * (META_BLK // tq) + (qi - nq_real)), g)

    return pl.pallas_call(
        functools.partial(_flash_kernel, nk=nk),
        grid=(B, GQ_KV_HEADS, nq, nk),
        in_specs=[pl.BlockSpec((tq, GQ_GROUP * GQ_HEAD), q_map),
                  pl.BlockSpec((tk, GQ_HEAD), lambda b, g, qi, ki: (b * nk + ki, g)),
                  pl.BlockSpec((tk, GQ_HEAD), lambda b, g, qi, ki: (b * nk + ki, g)),
                  pl.BlockSpec((META_BLK, GQ_HEAD), lambda b, g, qi, ki: (meta_k + b, g)),
                  pl.BlockSpec((META_BLK, GQ_HEAD), lambda b, g, qi, ki: (meta_k + b, g))],
        out_specs=pl.BlockSpec((tq, GQ_GROUP * GQ_HEAD), q_map),
        out_shape=jax.ShapeDtypeStruct((T, GQ_WIDTH), BF16),
        scratch_shapes=[pltpu.VMEM((GQ_GROUP * tq, 1), F32),
                        pltpu.VMEM((GQ_GROUP * tq, 1), F32),
                        pltpu.VMEM((GQ_GROUP * tq, GQ_HEAD), F32)],
        compiler_params=_cparams(("parallel", "parallel", "parallel", "arbitrary")),
        name="gqa_flash",
    )(q, k, v, k, v)


NA_QROWS = 8
NA_KROWS = 16
NA_QN = NA_QROWS * GRID_W
NA_KN = NA_KROWS * GRID_W


def _na_bias_tables(rpb):
    qi = np.arange(NA_QROWS)[:, None]
    kr = np.arange(NA_KROWS)[None, :]
    qj = np.arange(GRID_W)[:, None]
    kc = np.arange(GRID_W)[None, :]
    c0 = np.clip(qj - NA_KC // 2, 0, GRID_W - NA_KC)
    valid_c = (kc >= c0) & (kc < c0 + NA_KC)
    col_off = kc - qj + NA_KC - 1
    n_dr, n_dc = 2 * NA_KR - 1, 2 * NA_KC - 1
    col_sel = (col_off[None] == np.arange(n_dc)[:, None, None]) & valid_c[None]
    row_sel, row_ok = [], []
    for i_rel, r0_rel in ((qi, np.maximum(qi - NA_KR // 2, 0)),
                          (qi + NA_KR // 2, qi),
                          (qi + NA_KR, np.minimum(qi + NA_KR // 2, NA_KR))):
        valid_r = (kr >= r0_rel) & (kr < r0_rel + NA_KR)
        row_off = kr - i_rel + NA_KR - 1
        row_sel.append((row_off[..., None] == np.arange(n_dr)) & valid_r[..., None])
        row_ok.append(valid_r)
    row_sel = np.stack(row_sel).astype(np.float32)
    valid = np.stack(row_ok)[:, :, None, :, None] & valid_c[None, None, :, None, :]
    band = jnp.einsum("hrd,dqk->hrqk", rpb, col_sel.astype(np.float32), precision=HIGHEST)
    tab = jnp.einsum("vior,hrqk->hviqok", row_sel, band, precision=HIGHEST)
    tab = jnp.where(valid[None], tab, NEG)
    return tab.reshape(NA_HEADS, 3, NA_QN, NA_KN)


def _na_kernel(q_ref, k_ref, v_ref, km_ref, vm_ref, bias_ref, o_ref, *, rows):
    nblk = rows // NA_QROWS
    scale = NA_HEAD ** -0.5
    km = km_ref[...]
    vm = vm_ref[...]
    is_meta = lax.broadcasted_iota(jnp.int32, (1, META_BLK), 1) >= META_OFF

    def body(blk, carry):
        q = q_ref[pl.ds(pl.multiple_of(blk * NA_QN, NA_QN), NA_QN), :]
        krow = jnp.clip(blk * NA_QROWS - NA_KR // 2, 0, rows - NA_KROWS)
        ks = pl.multiple_of(krow * GRID_W, 4 * GRID_W)
        kw = k_ref[pl.ds(ks, NA_KN), :]
        vw = v_ref[pl.ds(ks, NA_KN), :]
        variant = jnp.where(blk == 0, 0, jnp.where(blk == nblk - 1, 2, 1))
        s = _dot_nt(q, kw) * scale + bias_ref[0, variant]
        sm = jnp.where(is_meta, _dot_nt(q, km) * scale, NEG)
        m = jnp.maximum(jnp.max(s, axis=-1, keepdims=True), jnp.max(sm, axis=-1, keepdims=True))
        p = jnp.exp(s - m)
        pm = jnp.exp(sm - m)
        l = jnp.sum(p, axis=-1, keepdims=True) + jnp.sum(pm, axis=-1, keepdims=True)
        o = (_dot(p.astype(BF16), vw) + _dot(pm.astype(BF16), vm)) / l
        o_ref[pl.ds(pl.multiple_of(blk * NA_QN, NA_QN), NA_QN), :] = o.astype(o_ref.dtype)
        return carry

    lax.fori_loop(0, nblk, body, 0)


def _na_meta_kernel(y_hbm, q_ref, km_ref, vm_ref, o_ref):
    del y_hbm
    is_meta = lax.broadcasted_iota(jnp.int32, (1, META_BLK), 1) >= META_OFF
    s = jnp.where(is_meta, _dot_nt(q_ref[...], km_ref[...]) * NA_HEAD ** -0.5, NEG)
    p = jnp.exp(s - jnp.max(s, axis=-1, keepdims=True))
    o = _dot(p.astype(BF16), vm_ref[...]) / jnp.sum(p, axis=-1, keepdims=True)
    o_ref[...] = o.astype(o_ref.dtype)


def _neighborhood_attention(p_na, rpb, B, S):
    T = p_na.shape[0]
    rows = S // GRID_W
    assert rows % NA_QROWS == 0 and rows >= NA_KROWS
    bias = _na_bias_tables(rpb)
    meta_blk = B * S // META_BLK
    y = pl.pallas_call(
        functools.partial(_na_kernel, rows=rows),
        grid=(B, NA_HEADS),
        in_specs=[pl.BlockSpec((S, NA_HEAD), lambda b, h: (b, h)),
                  pl.BlockSpec((S, NA_HEAD), lambda b, h: (b, NA_HEADS + h)),
                  pl.BlockSpec((S, NA_HEAD), lambda b, h: (b, 2 * NA_HEADS + h)),
                  pl.BlockSpec((META_BLK, NA_HEAD), lambda b, h: (meta_blk + b, NA_HEADS + h)),
                  pl.BlockSpec((META_BLK, NA_HEAD), lambda b, h: (meta_blk + b, 2 * NA_HEADS + h)),
                  pl.BlockSpec((1, 3, NA_QN, NA_KN), lambda b, h: (h, 0, 0, 0))],
        out_specs=pl.BlockSpec((S, NA_HEAD), lambda b, h: (b, h)),
        out_shape=jax.ShapeDtypeStruct((T, NA_WIDTH), BF16),
        compiler_params=_cparams(("parallel", "arbitrary")),
        name="na_window",
    )(p_na, p_na, p_na, p_na, p_na, bias)
    return pl.pallas_call(
        _na_meta_kernel,
        grid=(B, NA_HEADS),
        in_specs=[pl.BlockSpec(memory_space=pl.ANY),
                  pl.BlockSpec((META_BLK, NA_HEAD), lambda b, h: (meta_blk + b, h)),
                  pl.BlockSpec((META_BLK, NA_HEAD), lambda b, h: (meta_blk + b, NA_HEADS + h)),
                  pl.BlockSpec((META_BLK, NA_HEAD), lambda b, h: (meta_blk + b, 2 * NA_HEADS + h))],
        out_specs=pl.BlockSpec((META_BLK, NA_HEAD), lambda b, h: (meta_blk + b, h)),
        out_shape=jax.ShapeDtypeStruct((T, NA_WIDTH), BF16),
        input_output_aliases={0: 0},
        compiler_params=_cparams(("parallel", "arbitrary")),
        name="na_meta",
    )(y, p_na, p_na, p_na)


def _pair_sum_matrix(value):
    r = np.arange(LANE)[:, None] // RW_HEAD
    c = np.arange(LANE)[None, :] // RW_HEAD
    return jnp.asarray(np.where(r == c, value, 0.0), F32)


def _rw_prep_kernel(x_ref, pb_ref, nb_ref, mup_ref, mun_ref, w0_ref, wup_ref, a0_ref, aup_ref, gup_ref,
                    kk_w_ref, ka_ref, ones_ref,
                    r_ref, v_ref, kk_ref, lw_ref, kd_ref, bd_ref, g_ref, *, n_real_tiles, tiles_per_batch):
    i = pl.program_id(0)
    tl = x_ref.shape[0]
    C = RW_WIDTH
    is_meta = i >= n_real_tiles
    last = (i % tiles_per_batch) == tiles_per_batch - 1
    x = x_ref[...]
    rid = lax.broadcasted_iota(jnp.int32, (tl, 1), 0)
    prev = jnp.where(rid == 0, pb_ref[7:8, :], pltpu.roll(x, 1, axis=0))
    nxt = jnp.where(rid == tl - 1, nb_ref[0:1, :], pltpu.roll(x, tl - 1, axis=0))
    no_prev_upto = jnp.where(is_meta, META_OFF, -1)
    no_next_from = jnp.where(jnp.logical_and(jnp.logical_not(is_meta), last), tl - 1, tl)
    prev = jnp.where(rid <= no_prev_upto, 0.0, prev)
    nxt = jnp.where(rid >= no_next_from, 0.0, nxt)
    xs = x + mup_ref[...] * (prev - x) + mun_ref[...] * (nxt - x)

    live = rid >= jnp.where(is_meta, META_OFF, 0)
    r = xs[:, 0:C]
    k = xs[:, C:2 * C]
    v = xs[:, 2 * C:3 * C]
    o = 3 * C
    w_dn = xs[:, o:o + 2 * DECAY_LORA]
    a_dn = xs[:, o + 2 * DECAY_LORA:o + 2 * DECAY_LORA + 2 * ICL_LORA]
    g_dn = xs[:, o + 2 * DECAY_LORA + 2 * ICL_LORA:]
    w_log = w0_ref[...] + _dot(jnp.tanh(w_dn), wup_ref[...], HIGHEST)
    a = jax.nn.sigmoid(a0_ref[...] + _dot(a_dn, aup_ref[...], HIGHEST))
    g_ref[...] = _dot(jax.nn.sigmoid(g_dn), gup_ref[...], HIGHEST)
    logw = -math.exp(-0.5) * jax.nn.sigmoid(w_log)
    kk_raw = k * kk_w_ref[...]
    ka = ka_ref[...]
    ones = ones_ref[...]
    for hp in range(RW_PAIRS):
        sl = slice(hp * LANE, (hp + 1) * LANE)
        kr = kk_raw[:, sl]
        nrm = jnp.sqrt(_dot(kr * kr, ones, HIGHEST))
        kk = jnp.where(live, kr / jnp.maximum(nrm, 1e-12), 0.0)
        r_ref[hp] = jnp.where(live, r[:, sl], 0.0)
        v_ref[hp] = jnp.where(live, v[:, sl], 0.0)
        kk_ref[hp] = kk
        kp = k[:, sl]
        for d in range(2):
            sld = slice(d * C + hp * LANE, d * C + (hp + 1) * LANE)
            ad = a[:, sld]
            lw_ref[d, hp] = jnp.where(live, logw[:, sld], 0.0)
            kd_ref[d, hp] = jnp.where(live, kp * (1.0 + (ad - 1.0) * ka[:, sl]), 0.0)
            bd_ref[d, hp] = kk * ad


def _rw_prep(p_rw, prm, B, S):
    T = p_rw.shape[0]
    tl = PREP_ROWS
    n_real_tiles = B * S // tl
    tiles_per_batch = S // tl
    sub = tl // 8

    def prev_map(i):
        b = i // tiles_per_batch
        meta_last = (B * S + META_BLK * b + META_BLK - 8) // 8
        real = jnp.where(i % tiles_per_batch == 0, meta_last, i * sub - 1)
        return (jnp.where(i < n_real_tiles, real, jnp.maximum(i * sub - 1, 0)), 0)

    def next_map(i):
        real = jnp.where(i % tiles_per_batch == tiles_per_batch - 1, 0, (i + 1) * sub)
        return (jnp.where(i < n_real_tiles, real, (i - n_real_tiles) * (S // 8)), 0)

    fix = lambda i: (0, 0)
    pair = pl.BlockSpec((RW_PAIRS, tl, LANE), lambda i: (0, i, 0))
    pair2 = pl.BlockSpec((2, RW_PAIRS, tl, LANE), lambda i: (0, 0, i, 0))
    pair_shape = jax.ShapeDtypeStruct((RW_PAIRS, T, LANE), F32)
    pair2_shape = jax.ShapeDtypeStruct((2, RW_PAIRS, T, LANE), F32)
    C = RW_WIDTH
    return pl.pallas_call(
        functools.partial(_rw_prep_kernel, n_real_tiles=n_real_tiles, tiles_per_batch=tiles_per_batch),
        grid=(T // tl,),
        in_specs=[pl.BlockSpec((tl, RW_PROJ_PAD), lambda i: (i, 0)),
                  pl.BlockSpec((8, RW_PROJ_PAD), prev_map),
                  pl.BlockSpec((8, RW_PROJ_PAD), next_map),
                  pl.BlockSpec((1, RW_PROJ_PAD), fix),
                  pl.BlockSpec((1, RW_PROJ_PAD), fix),
                  pl.BlockSpec((1, 2 * C), fix),
                  pl.BlockSpec((2 * DECAY_LORA, 2 * C), fix),
                  pl.BlockSpec((1, 2 * C), fix),
                  pl.BlockSpec((2 * ICL_LORA, 2 * C), fix),
                  pl.BlockSpec((GATE_PAD, C), fix),
                  pl.BlockSpec((1, C), fix),
                  pl.BlockSpec((1, C), fix),
                  pl.BlockSpec((LANE, LANE), fix)],
        out_specs=[pair, pair, pair, pair2, pair2, pair2, pl.BlockSpec((tl, C), lambda i: (i, 0))],
        out_shape=[pair_shape, pair_shape, pair_shape, pair2_shape, pair2_shape, pair2_shape,
                   jax.ShapeDtypeStruct((T, C), F32)],
        compiler_params=_cparams(("parallel",)),
        name="rwkv_prep",
    )(p_rw, p_rw, p_rw, prm["mu_prev"], prm["mu_next"], prm["w0"], prm["w_up"], prm["a0"], prm["a_up"],
      prm["g_up"], prm["k_k"], prm["k_a"], _pair_sum_matrix(1.0))


SCAN_GROUP = 4


def _rw_scan_kernel(r_ref, v_ref, kk_ref, lw_ref, kd_ref, bd_ref, o_ref, s_sc):
    d = pl.program_id(0) % 2
    c = pl.program_id(1)

    @pl.when(c == 0)
    def _():
        s_sc[...] = jnp.zeros(s_sc.shape, F32)

    P2 = 2 * CHUNK
    sign = 1 - 2 * d
    row = lax.broadcasted_iota(jnp.int32, (CHUNK, CHUNK), 0)
    col = lax.broadcasted_iota(jnp.int32, (CHUNK, CHUNK), 1)
    tri = jnp.where((row - col) * sign >= 0, 1.0, 0.0)
    ri = lax.broadcasted_iota(jnp.int32, (P2, P2), 0)
    ci = lax.broadcasted_iota(jnp.int32, (P2, P2), 1)
    same = (ri // CHUNK) == (ci // CHUNK)
    lag = (ri % CHUNK - ci % CHUNK) * sign
    strict = jnp.logical_and(same, lag > 0)
    eye = jnp.where(ri == ci, 1.0, 0.0)
    ri2 = lax.broadcasted_iota(jnp.int32, (P2, 2 * P2), 0)
    ci2 = lax.broadcasted_iota(jnp.int32, (P2, 2 * P2), 1) % P2
    incl2 = jnp.logical_and((ri2 // CHUNK) == (ci2 // CHUNK), (ri2 % CHUNK - ci2 % CHUNK) * sign >= 0)
    head0 = lax.broadcasted_iota(jnp.int32, (CHUNK, P2), 1) < RW_HEAD
    n_levels = int(math.log2(CHUNK)) - 1

    def stack(z):
        return jnp.concatenate([jnp.where(head0, z, 0.0), jnp.where(head0, 0.0, z)], axis=0)

    def prep(hp):
        lw = lw_ref[0, hp]
        cum = _dot(tri, lw, HIGHEST)
        p_inv = jnp.exp(-cum)
        x = jnp.concatenate([stack(-kk_ref[hp] * jnp.exp(cum - lw)),
                             stack(r_ref[hp] * jnp.exp(cum))], axis=0).astype(BF16)
        y = jnp.concatenate([stack(bd_ref[0, hp] * p_inv),
                             stack(kd_ref[0, hp] * p_inv)], axis=0).astype(BF16)
        vbd = stack(v_ref[hp]).astype(BF16)
        p_tot = jnp.exp(jnp.sum(lw, axis=0, keepdims=True))
        return x, y, vbd, p_tot

    for g0 in range(0, RW_PAIRS, SCAN_GROUP):
        hps = list(range(g0, g0 + SCAN_GROUP))
        pre = [prep(hp) for hp in hps]
        xs_ = [p[0] for p in pre]
        ys_ = [p[1] for p in pre]
        vs_ = [p[2] for p in pre]
        s_old = [s_sc[hp] for hp in hps]
        gram = [_dot_nt(x, y) for x, y in zip(xs_, ys_)]
        xst = [_dot_nt(x, s.astype(BF16)) for x, s in zip(xs_, s_old)]
        a_ab = [jnp.where(strict, g[:P2, :P2], 0.0) for g in gram]
        t_inv = [eye + a for a in a_ab]
        pw = [a.astype(BF16) for a in a_ab]
        for _ in range(n_levels):
            pw = [_dot(p, p).astype(BF16) for p in pw]
            t_inv = [t + _dot(t.astype(BF16), p) for t, p in zip(t_inv, pw)]
        rhs = [xt[:P2] + _dot(jnp.where(strict, g[:P2, P2:], 0.0).astype(BF16), vb)
               for xt, g, vb in zip(xst, gram, vs_)]
        u = [_dot(t.astype(BF16), r_.astype(BF16)) for t, r_ in zip(t_inv, rhs)]
        uv = [jnp.concatenate([u_.astype(BF16), vb], axis=0) for u_, vb in zip(u, vs_)]
        for i, hp in enumerate(hps):
            a_r = jnp.where(incl2, gram[i][P2:, :], 0.0).astype(BF16)
            o_bd = xst[i][P2:] + _dot(a_r, uv[i])
            o_ref[0, hp] = o_bd[:CHUNK] + o_bd[CHUNK:]
            s_sc[hp] = (s_old[i] + _dot_tn(uv[i], ys_[i])) * pre[i][3]


def _rw_scan(r, v, kk, lw, kd, bd, B, S):
    T = r.shape[1]
    nc = S // CHUNK + 1
    meta_chunk = (B * S + META_BLK - CHUNK) // CHUNK

    def blk(g, c):
        b = g // 2
        j = jnp.where(g % 2 == 0, c, nc - 1 - c)
        return jnp.where(j == 0, meta_chunk + b * (META_BLK // CHUNK), b * (S // CHUNK) + j - 1)

    shared = pl.BlockSpec((RW_PAIRS, CHUNK, LANE), lambda g, c: (0, blk(g, c), 0))
    per_dir = pl.BlockSpec((1, RW_PAIRS, CHUNK, LANE), lambda g, c: (g % 2, 0, blk(g, c), 0))
    return pl.pallas_call(
        _rw_scan_kernel,
        grid=(2 * B, nc),
        in_specs=[shared, shared, shared, per_dir, per_dir, per_dir],
        out_specs=per_dir,
        out_shape=jax.ShapeDtypeStruct((2, RW_PAIRS, T, LANE), F32),
        scratch_shapes=[pltpu.VMEM((RW_PAIRS, LANE, LANE), F32)],
        compiler_params=_cparams(("parallel", "arbitrary")),
        name="rwkv_scan",
    )(r, v, kk, lw, kd, bd)


def _rw_post_kernel(o_ref, r_ref, v_ref, kd_ref, g_ref, rk_ref, gw_ref, gb_ref, mean_ref, ones_ref, y_ref):
    mean_m = mean_ref[...]
    ones = ones_ref[...]
    for hp in range(RW_PAIRS):
        sl = slice(hp * LANE, (hp + 1) * LANE)
        y = o_ref[0, hp] + o_ref[1, hp]
        yc = y - _dot(y, mean_m, HIGHEST)
        var = _dot(yc * yc, mean_m, HIGHEST)
        yn = yc * lax.rsqrt(var + GN_EPS) * gw_ref[:, sl] + gb_ref[:, sl]
        k_bonus = 0.5 * (kd_ref[0, hp] + kd_ref[1, hp])
        bonus = _dot(r_ref[hp] * k_bonus * rk_ref[:, sl], ones, HIGHEST) * v_ref[hp]
        y_ref[:, sl] = ((yn + bonus) * g_ref[:, sl]).astype(y_ref.dtype)


def _rw_post(o, r, v, kd, g, prm):
    T = g.shape[0]
    tl = _row_tile(T, 256)
    C = RW_WIDTH
    fix = lambda i: (0, 0)
    pair = pl.BlockSpec((RW_PAIRS, tl, LANE), lambda i: (0, i, 0))
    pair2 = pl.BlockSpec((2, RW_PAIRS, tl, LANE), lambda i: (0, 0, i, 0))
    return pl.pallas_call(
        _rw_post_kernel,
        grid=(T // tl,),
        in_specs=[pair2, pair, pair, pair2,
                  pl.BlockSpec((tl, C), lambda i: (i, 0)),
                  pl.BlockSpec((1, C), fix), pl.BlockSpec((1, C), fix), pl.BlockSpec((1, C), fix),
                  pl.BlockSpec((LANE, LANE), fix), pl.BlockSpec((LANE, LANE), fix)],
        out_specs=pl.BlockSpec((tl, C), lambda i: (i, 0)),
        out_shape=jax.ShapeDtypeStruct((T, C), BF16),
        compiler_params=_cparams(("parallel",)),
        name="rwkv_post",
    )(o, r, v, kd, g, prm["r_k"], prm["gn_w"], prm["gn_b"],
      _pair_sum_matrix(1.0 / RW_HEAD), _pair_sum_matrix(1.0))


def _block_diag2(m):
    z = jnp.zeros_like(m[0])
    return jnp.concatenate([jnp.concatenate([m[0], z], axis=1), jnp.concatenate([z, m[1]], axis=1)], axis=0)


def _rwkv7(p_rw, mu_prev, mu_next, w0, w_up, a0, a_up, g_up, k_k, k_a, r_k, gn_w, gn_b, B, S):
    C = RW_WIDTH
    pad = RW_PROJ_PAD - RW_PROJ
    prm = {
        "mu_prev": jnp.pad(mu_prev, (0, pad)).reshape(1, RW_PROJ_PAD),
        "mu_next": jnp.pad(mu_next, (0, pad)).reshape(1, RW_PROJ_PAD),
        "w0": w0.reshape(1, 2 * C),
        "w_up": _block_diag2(w_up),
        "a0": a0.reshape(1, 2 * C),
        "a_up": _block_diag2(a_up),
        "g_up": jnp.pad(g_up, ((0, GATE_PAD - GATE_LORA), (0, 0))),
        "k_k": k_k.reshape(1, C),
        "k_a": k_a.reshape(1, C),
        "r_k": r_k.reshape(1, C),
        "gn_w": gn_w.reshape(1, C),
        "gn_b": gn_b.reshape(1, C),
    }
    r, v, kk, lw, kd, bd, g = _rw_prep(p_rw, prm, B, S)
    o = _rw_scan(r, v, kk, lw, kd, bd, B, S)
    return _rw_post(o, r, v, kd, g, prm)


def kernel(x, meta_tokens, ffn1_w_in, ffn1_w_out, ln_ffn1_g, ln_ffn1_b, w_in, rw_mu_prev, rw_mu_next, rw_w0, rw_w_up, rw_a0, rw_a_up, rw_g_up, rw_k_k, rw_k_a, rw_r_k, rw_gn_w, rw_gn_b, na_rpb, gq_q_gain, gq_k_gain, w_out, ln_mix_g, ln_mix_b, ffn2_w_in, ffn2_w_out, ln_ffn2_g, ln_ffn2_b):
    B, S, D = x.shape
    assert D == D_MODEL and S % 1024 == 0
    meta = jnp.zeros((B, META_BLK, D), x.dtype).at[:, META_OFF:].set(
        jnp.broadcast_to(meta_tokens[None].astype(x.dtype), (B, N_META, D)))
    h = jnp.concatenate([x.reshape(B * S, D), meta.reshape(B * META_BLK, D)], axis=0)
    hb = h.astype(BF16)
    cos_t, sin_t = _rope_tables(B, S)
    for l in range(DEPTH):
        h, hb = _ffn(h, hb, ffn1_w_in[l], ffn1_w_out[l], ln_ffn1_g[l], ln_ffn1_b[l])
        w = w_in[l]
        w_rw = jnp.pad(w[:, :RW_PROJ], ((0, 0), (0, RW_PROJ_PAD - RW_PROJ))).astype(BF16)
        w_na = w[:, RW_PROJ:RW_PROJ + NA_PROJ].astype(BF16)
        w_gq = w[:, RW_PROJ + NA_PROJ:].astype(BF16)
        p_rw = _matmul(hb, w_rw, F32)
        p_na = _matmul(hb, w_na, BF16)
        p_gq = _matmul(hb, w_gq, F32)
        y_a = _rwkv7(p_rw, rw_mu_prev[l], rw_mu_next[l], rw_w0[l], rw_w_up[l], rw_a0[l], rw_a_up[l],
                     rw_g_up[l], rw_k_k[l], rw_k_a[l], rw_r_k[l], rw_gn_w[l], rw_gn_b[l], B, S)
        y_b = _neighborhood_attention(p_na, na_rpb[l], B, S)
        q, k, v = _gq_prep(p_gq, cos_t, sin_t, gq_q_gain[l], gq_k_gain[l])
        y_c = _gqa(q, k, v, B, S)
        y = jnp.concatenate([y_a, y_b, y_c], axis=-1)
        z = _matmul_resid(y, w_out[l].astype(BF16), h, 1.0)
        h, hb = _layer_norm(z, ln_mix_g[l], ln_mix_b[l])
        h, hb = _ffn(h, hb, ffn2_w_in[l], ffn2_w_out[l], ln_ffn2_g[l], ln_ffn2_b[l])
    return h[:B * S].reshape(B, S, D)
```

```python
import functools
import math

import numpy as np
import jax
import jax.numpy as jnp
from jax import lax
from jax.experimental import pallas as pl
from jax.experimental.pallas import tpu as pltpu

F32 = jnp.float32
BF16 = jnp.bfloat16
HIGHEST = lax.Precision.HIGHEST

D_MODEL = 4096
DEPTH = 4
N_META = 16
GRID_W = 64
RW_HEAD = 64
RW_WIDTH = 1536
RW_HEADS = RW_WIDTH // RW_HEAD
RW_PAIRS = RW_HEADS // 2
DECAY_LORA = 64
ICL_LORA = 64
GATE_LORA = 224
RW_PROJ = 3 * RW_WIDTH + 2 * DECAY_LORA + 2 * ICL_LORA + GATE_LORA
GN_EPS = 64e-5
NA_HEAD = 128
NA_WIDTH = 1536
NA_HEADS = NA_WIDTH // NA_HEAD
NA_KR = 8
NA_KC = 16
NA_PROJ = 3 * NA_WIDTH
GQ_HEAD = 128
GQ_WIDTH = 1024
GQ_HEADS = GQ_WIDTH // GQ_HEAD
GQ_KV_HEADS = 2
GQ_GROUP = GQ_HEADS // GQ_KV_HEADS
GQ_KV_W = GQ_KV_HEADS * GQ_HEAD
GQ_PROJ = GQ_WIDTH + 2 * GQ_KV_W
ROPE_THETA = 10000.0
LOG2E = math.log2(math.e)
QK_EPS = 1e-6
D_FF = 3 * D_MODEL // 2
LN_EPS = 1e-5
ALPHA = (2.0 * DEPTH) ** 0.25

LANE = 128
META_BLK = 128
META_OFF = META_BLK - N_META
RW_PROJ_PAD = 5120
GATE_PAD = 256
CHUNK = 64
PREP_ROWS = 128
NEG = -1e30
VMEM_LIMIT = 56 * 2 ** 20


def _cparams(sem):
    return pltpu.CompilerParams(dimension_semantics=sem, vmem_limit_bytes=VMEM_LIMIT)


def _row_tile(total, cap):
    best = LANE
    for t in range(LANE, cap + 1, LANE):
        if total % t == 0:
            best = t
    return best


def _dot_nt(a, b, precision=None):
    return lax.dot_general(a, b, (((1,), (1,)), ((), ())), preferred_element_type=F32, precision=precision)


def _dot_tn(a, b, precision=None):
    return lax.dot_general(a, b, (((0,), (0,)), ((), ())), preferred_element_type=F32, precision=precision)


def _dot(a, b, precision=None):
    return jnp.dot(a, b, preferred_element_type=F32, precision=precision)


def _mm_kernel(x_ref, w_ref, o_ref):
    o_ref[...] = _dot(x_ref[...], w_ref[...]).astype(o_ref.dtype)


def _matmul(x, w, out_dtype, tn=512):
    T, K = x.shape
    N = w.shape[1]
    tm = _row_tile(T, 1280)
    return pl.pallas_call(
        _mm_kernel,
        grid=(T // tm, N // tn),
        in_specs=[pl.BlockSpec((tm, K), lambda i, j: (i, 0)),
                  pl.BlockSpec((K, tn), lambda i, j: (0, j))],
        out_specs=pl.BlockSpec((tm, tn), lambda i, j: (i, j)),
        out_shape=jax.ShapeDtypeStruct((T, N), out_dtype),
        compiler_params=_cparams(("parallel", "arbitrary")),
        name="proj_matmul",
    )(x, w)


def _swiglu_kernel(x_ref, wg_ref, wu_ref, o_ref):
    x = x_ref[...]
    g = _dot(x, wg_ref[...])
    u = _dot(x, wu_ref[...])
    o_ref[...] = (g * jax.nn.sigmoid(g) * u).astype(o_ref.dtype)


def _swiglu_matmul(x, w_in, tn=512):
    T, K = x.shape
    F = w_in.shape[1] // 2
    tm = _row_tile(T, 1280)
    nj = F // tn
    return pl.pallas_call(
        _swiglu_kernel,
        grid=(T // tm, nj),
        in_specs=[pl.BlockSpec((tm, K), lambda i, j: (i, 0)),
                  pl.BlockSpec((K, tn), lambda i, j: (0, j)),
                  pl.BlockSpec((K, tn), lambda i, j: (0, j + nj))],
        out_specs=pl.BlockSpec((tm, tn), lambda i, j: (i, j)),
        out_shape=jax.ShapeDtypeStruct((T, F), BF16),
        compiler_params=_cparams(("parallel", "arbitrary")),
        name="ffn_swiglu",
    )(x, w_in, w_in)


def _mm_resid_kernel(x_ref, w_ref, h_ref, o_ref, *, scale):
    o_ref[...] = ALPHA * h_ref[...] + scale * _dot(x_ref[...], w_ref[...])


def _matmul_resid(x, w, h, scale, tn=512):
    T, K = x.shape
    N = w.shape[1]
    tm = _row_tile(T, 640)
    return pl.pallas_call(
        functools.partial(_mm_resid_kernel, scale=scale),
        grid=(T // tm, N // tn),
        in_specs=[pl.BlockSpec((tm, K), lambda i, j: (i, 0)),
                  pl.BlockSpec((K, tn), lambda i, j: (0, j)),
                  pl.BlockSpec((tm, tn), lambda i, j: (i, j))],
        out_specs=pl.BlockSpec((tm, tn), lambda i, j: (i, j)),
        out_shape=jax.ShapeDtypeStruct((T, N), F32),
        compiler_params=_cparams(("parallel", "arbitrary")),
        name="resid_matmul",
    )(x, w, h)


def _mix_out_kernel(ya_ref, yb_ref, yc_ref, wa_ref, wb_ref, wc_ref, h_ref, o_ref):
    mix = _dot(ya_ref[...], wa_ref[...]) + _dot(yb_ref[...], wb_ref[...]) + _dot(yc_ref[...], wc_ref[...])
    o_ref[...] = ALPHA * h_ref[...] + mix


def _mix_out_matmul(y_a, y_b, y_c, w, h, tn=512):
    T = h.shape[0]
    N = w.shape[1]
    assert RW_WIDTH == NA_WIDTH and (RW_WIDTH + NA_WIDTH) % GQ_WIDTH == 0
    tm = _row_tile(T, 640)
    row = lambda i, j: (i, 0)
    return pl.pallas_call(
        _mix_out_kernel,
        grid=(T // tm, N // tn),
        in_specs=[pl.BlockSpec((tm, RW_WIDTH), row),
                  pl.BlockSpec((tm, NA_WIDTH), row),
                  pl.BlockSpec((tm, GQ_WIDTH), row),
                  pl.BlockSpec((RW_WIDTH, tn), lambda i, j: (0, j)),
                  pl.BlockSpec((NA_WIDTH, tn), lambda i, j: (1, j)),
                  pl.BlockSpec((GQ_WIDTH, tn), lambda i, j: ((RW_WIDTH + NA_WIDTH) // GQ_WIDTH, j)),
                  pl.BlockSpec((tm, tn), lambda i, j: (i, j))],
        out_specs=pl.BlockSpec((tm, tn), lambda i, j: (i, j)),
        out_shape=jax.ShapeDtypeStruct((T, N), F32),
        compiler_params=_cparams(("parallel", "arbitrary")),
        name="mix_out_matmul",
    )(y_a, y_b, y_c, w, w, w, h)


def _ln_kernel(z_ref, g_ref, b_ref, h_ref, hb_ref):
    z = z_ref[...]
    mu = jnp.mean(z, axis=-1, keepdims=True)
    zc = z - mu
    var = jnp.mean(zc * zc, axis=-1, keepdims=True)
    y = zc * lax.rsqrt(var + LN_EPS) * g_ref[...] + b_ref[...]
    h_ref[...] = y
    hb_ref[...] = y.astype(BF16)


def _layer_norm(z, g, b):
    T, D = z.shape
    tm = _row_tile(T, 256)
    return pl.pallas_call(
        _ln_kernel,
        grid=(T // tm,),
        in_specs=[pl.BlockSpec((tm, D), lambda i: (i, 0)),
                  pl.BlockSpec((1, D), lambda i: (0, 0)),
                  pl.BlockSpec((1, D), lambda i: (0, 0))],
        out_specs=[pl.BlockSpec((tm, D), lambda i: (i, 0)),
                   pl.BlockSpec((tm, D), lambda i: (i, 0))],
        out_shape=[jax.ShapeDtypeStruct((T, D), F32), jax.ShapeDtypeStruct((T, D), BF16)],
        compiler_params=_cparams(("parallel",)),
        name="layer_norm",
    )(z, g.reshape(1, D), b.reshape(1, D))


def _ffn(h, hb, w_in, w_out, g, b):
    act = _swiglu_matmul(hb, w_in.astype(BF16))
    z = _matmul_resid(act, w_out.astype(BF16), h, 0.5)
    return _layer_norm(z, g, b)


def _rope_tables(B, S):
    t = jnp.arange(S)
    row = (t // GRID_W).astype(F32)
    col = (t % GRID_W).astype(F32)
    half = GQ_HEAD // 4
    inv = ROPE_THETA ** (-jnp.arange(half, dtype=F32) / half)
    ang = jnp.concatenate([row[:, None] * inv, col[:, None] * inv], axis=-1)
    cos, sin = jnp.cos(ang), jnp.sin(ang)
    cos2 = jnp.concatenate([cos, cos], axis=-1)
    sin2 = jnp.concatenate([-sin, sin], axis=-1)
    pad_c = jnp.ones((B * META_BLK, GQ_HEAD), F32)
    pad_s = jnp.zeros((B * META_BLK, GQ_HEAD), F32)
    return (jnp.concatenate([jnp.tile(cos2, (B, 1)), pad_c], axis=0),
            jnp.concatenate([jnp.tile(sin2, (B, 1)), pad_s], axis=0))


def _gq_prep_kernel(p_ref, cos_ref, sin_ref, qg_ref, kg_ref, q_ref, k_ref, v_ref):
    cos = cos_ref[...]
    sin = sin_ref[...]

    def rms_rope(x, gain):
        x = x * lax.rsqrt(jnp.mean(x * x, axis=-1, keepdims=True) + QK_EPS) * gain
        return x * cos + pltpu.roll(x, GQ_HEAD // 2, axis=1) * sin

    for hd in range(GQ_HEADS):
        sl = slice(hd * GQ_HEAD, (hd + 1) * GQ_HEAD)
        q_ref[:, sl] = (rms_rope(p_ref[:, sl], qg_ref[...]) * (GQ_HEAD ** -0.5 * LOG2E)).astype(BF16)
    for hd in range(GQ_KV_HEADS):
        sl = slice(hd * GQ_HEAD, (hd + 1) * GQ_HEAD)
        slk = slice(GQ_WIDTH + hd * GQ_HEAD, GQ_WIDTH + (hd + 1) * GQ_HEAD)
        slv = slice(GQ_WIDTH + GQ_KV_W + hd * GQ_HEAD, GQ_WIDTH + GQ_KV_W + (hd + 1) * GQ_HEAD)
        k_ref[:, sl] = rms_rope(p_ref[:, slk], kg_ref[...]).astype(BF16)
        v_ref[:, sl] = p_ref[:, slv].astype(BF16)


def _gq_prep(p_gq, cos_t, sin_t, q_gain, k_gain):
    T = p_gq.shape[0]
    tm = _row_tile(T, 256)
    row = lambda i: (i, 0)
    fix = lambda i: (0, 0)
    return pl.pallas_call(
        _gq_prep_kernel,
        grid=(T // tm,),
        in_specs=[pl.BlockSpec((tm, GQ_PROJ), row),
                  pl.BlockSpec((tm, GQ_HEAD), row),
                  pl.BlockSpec((tm, GQ_HEAD), row),
                  pl.BlockSpec((1, GQ_HEAD), fix),
                  pl.BlockSpec((1, GQ_HEAD), fix)],
        out_specs=[pl.BlockSpec((tm, GQ_WIDTH), row),
                   pl.BlockSpec((tm, GQ_KV_W), row),
                   pl.BlockSpec((tm, GQ_KV_W), row)],
        out_shape=[jax.ShapeDtypeStruct((T, GQ_WIDTH), BF16),
                   jax.ShapeDtypeStruct((T, GQ_KV_W), BF16),
                   jax.ShapeDtypeStruct((T, GQ_KV_W), BF16)],
        compiler_params=_cparams(("parallel",)),
        name="gqa_prep",
    )(p_gq, cos_t, sin_t, q_gain.reshape(1, GQ_HEAD), k_gain.reshape(1, GQ_HEAD))


def _flash_kernel(q_ref, k_ref, v_ref, km_ref, vm_ref, o_ref, m_sc, acc_sc, *, nk):
    ki = pl.program_id(3)
    qs = [q_ref[:, g * GQ_HEAD:(g + 1) * GQ_HEAD] for g in range(GQ_GROUP)]

    def attend(k, v, mask):
        v_ones = jnp.concatenate([v, jnp.ones(v.shape, BF16)], axis=1)

        def scores(g):
            s = _dot_nt(qs[g], k)
            return s if mask is None else jnp.where(mask, s, NEG)

        def update(g, s):
            m_prev = m_sc[g]
            m_new = jnp.maximum(m_prev, jnp.max(s, axis=-1, keepdims=True))
            p = jnp.exp2(s - m_new).astype(BF16)
            acc_sc[g] = jnp.exp2(m_prev - m_new) * acc_sc[g] + _dot(p, v_ones)
            m_sc[g] = m_new

        s_next = scores(0)
        for g in range(GQ_GROUP):
            s_cur = s_next
            if g + 1 < GQ_GROUP:
                s_next = scores(g + 1)
            update(g, s_cur)

    @pl.when(ki == 0)
    def _():
        m_sc[...] = jnp.full(m_sc.shape, NEG, F32)
        acc_sc[...] = jnp.zeros(acc_sc.shape, F32)
        is_meta = lax.broadcasted_iota(jnp.int32, (1, META_BLK), 1) >= META_OFF
        attend(km_ref[...], vm_ref[...], is_meta)

    attend(k_ref[...], v_ref[...], None)

    @pl.when(ki == nk - 1)
    def _():
        for g in range(GQ_GROUP):
            o = acc_sc[g, :, :GQ_HEAD] / acc_sc[g, :, GQ_HEAD:]
            o_ref[:, g * GQ_HEAD:(g + 1) * GQ_HEAD] = o.astype(o_ref.dtype)


def _gqa(q, k, v, B, S):
    T = q.shape[0]
    tq = 128
    tk = _row_tile(S, 2048)
    nq_real = S // tq
    nq = nq_real + META_BLK // tq
    nk = S // tk
    meta_q = B * S // tq
    meta_k = B * S // META_BLK

    def q_map(b, g, qi, ki):
        return (jnp.where(qi < nq_real, b * nq_real + qi, meta_q + b * (META_BLK // tq) + (qi - nq_real)), g)

    return pl.pallas_call(
        functools.partial(_flash_kernel, nk=nk),
        grid=(B, GQ_KV_HEADS, nq, nk),
        in_specs=[pl.BlockSpec((tq, GQ_GROUP * GQ_HEAD), q_map),
                  pl.BlockSpec((tk, GQ_HEAD), lambda b, g, qi, ki: (b * nk + ki, g)),
                  pl.BlockSpec((tk, GQ_HEAD), lambda b, g, qi, ki: (b * nk + ki, g)),
                  pl.BlockSpec((META_BLK, GQ_HEAD), lambda b, g, qi, ki: (meta_k + b, g)),
                  pl.BlockSpec((META_BLK, GQ_HEAD), lambda b, g, qi, ki: (meta_k + b, g))],
        out_specs=pl.BlockSpec((tq, GQ_GROUP * GQ_HEAD), q_map),
        out_shape=jax.ShapeDtypeStruct((T, GQ_WIDTH), BF16),
        scratch_shapes=[pltpu.VMEM((GQ_GROUP, tq, 1), F32),
                        pltpu.VMEM((GQ_GROUP, tq, 2 * GQ_HEAD), F32)],
        compiler_params=_cparams(("parallel", "parallel", "parallel", "arbitrary")),
        name="gqa_flash",
    )(q, k, v, k, v)


NA_QROWS = 8
NA_KROWS = 16
NA_QN = NA_QROWS * GRID_W
NA_KN = NA_KROWS * GRID_W


def _na_bias_tables(rpb):
    qi = np.arange(NA_QROWS)[:, None]
    kr = np.arange(NA_KROWS)[None, :]
    qj = np.arange(GRID_W)[:, None]
    kc = np.arange(GRID_W)[None, :]
    c0 = np.clip(qj - NA_KC // 2, 0, GRID_W - NA_KC)
    valid_c = (kc >= c0) & (kc < c0 + NA_KC)
    col_off = kc - qj + NA_KC - 1
    n_dr, n_dc = 2 * NA_KR - 1, 2 * NA_KC - 1
    col_sel = (col_off[None] == np.arange(n_dc)[:, None, None]) & valid_c[None]
    row_sel, row_ok = [], []
    for i_rel, r0_rel in ((qi, np.maximum(qi - NA_KR // 2, 0)),
                          (qi + NA_KR // 2, qi),
                          (qi + NA_KR, np.minimum(qi + NA_KR // 2, NA_KR))):
        valid_r = (kr >= r0_rel) & (kr < r0_rel + NA_KR)
        row_off = kr - i_rel + NA_KR - 1
        row_sel.append((row_off[..., None] == np.arange(n_dr)) & valid_r[..., None])
        row_ok.append(valid_r)
    row_sel = np.stack(row_sel).astype(np.float32)
    valid = np.stack(row_ok)[:, :, None, :, None] & valid_c[None, None, :, None, :]
    band = jnp.einsum("hrd,dqk->hrqk", rpb, col_sel.astype(np.float32), precision=HIGHEST)
    tab = jnp.einsum("vior,hrqk->hviqok", row_sel, band, precision=HIGHEST)
    tab = jnp.where(valid[None], tab, NEG)
    return tab.reshape(NA_HEADS, 3, NA_QN, NA_KN)


def _na_kernel(q_ref, k_ref, v_ref, km_ref, vm_ref, bias_ref, o_ref, *, rows):
    nblk = rows // NA_QROWS
    scale = NA_HEAD ** -0.5
    km = km_ref[...]
    vm = vm_ref[...]
    is_meta = lax.broadcasted_iota(jnp.int32, (1, META_BLK), 1) >= META_OFF

    def body(blk, carry):
        q = q_ref[pl.ds(pl.multiple_of(blk * NA_QN, NA_QN), NA_QN), :]
        krow = jnp.clip(blk * NA_QROWS - NA_KR // 2, 0, rows - NA_KROWS)
        ks = pl.multiple_of(krow * GRID_W, 4 * GRID_W)
        kw = k_ref[pl.ds(ks, NA_KN), :]
        vw = v_ref[pl.ds(ks, NA_KN), :]
        variant = jnp.where(blk == 0, 0, jnp.where(blk == nblk - 1, 2, 1))
        s = _dot_nt(q, kw) * scale + bias_ref[0, variant]
        sm = jnp.where(is_meta, _dot_nt(q, km) * scale, NEG)
        m = jnp.maximum(jnp.max(s, axis=-1, keepdims=True), jnp.max(sm, axis=-1, keepdims=True))
        p = jnp.exp(s - m)
        pm = jnp.exp(sm - m)
        l = jnp.sum(p, axis=-1, keepdims=True) + jnp.sum(pm, axis=-1, keepdims=True)
        o = (_dot(p.astype(BF16), vw) + _dot(pm.astype(BF16), vm)) / l
        o_ref[pl.ds(pl.multiple_of(blk * NA_QN, NA_QN), NA_QN), :] = o.astype(o_ref.dtype)
        return carry

    lax.fori_loop(0, nblk, body, 0)


def _na_meta_kernel(q_ref, km_ref, vm_ref, o_ref):
    is_meta = lax.broadcasted_iota(jnp.int32, (1, META_BLK), 1) >= META_OFF
    s = jnp.where(is_meta, _dot_nt(q_ref[...], km_ref[...]) * NA_HEAD ** -0.5, NEG)
    p = jnp.exp(s - jnp.max(s, axis=-1, keepdims=True))
    o = _dot(p.astype(BF16), vm_ref[...]) / jnp.sum(p, axis=-1, keepdims=True)
    o_ref[...] = o.astype(o_ref.dtype)


def _neighborhood_attention(p_na, rpb, B, S):
    rows = S // GRID_W
    assert rows % NA_QROWS == 0 and rows >= NA_KROWS
    bias = _na_bias_tables(rpb)
    meta_blk = B * S // META_BLK
    y = pl.pallas_call(
        functools.partial(_na_kernel, rows=rows),
        grid=(B, NA_HEADS),
        in_specs=[pl.BlockSpec((S, NA_HEAD), lambda b, h: (b, h)),
                  pl.BlockSpec((S, NA_HEAD), lambda b, h: (b, NA_HEADS + h)),
                  pl.BlockSpec((S, NA_HEAD), lambda b, h: (b, 2 * NA_HEADS + h)),
                  pl.BlockSpec((META_BLK, NA_HEAD), lambda b, h: (meta_blk + b, NA_HEADS + h)),
                  pl.BlockSpec((META_BLK, NA_HEAD), lambda b, h: (meta_blk + b, 2 * NA_HEADS + h)),
                  pl.BlockSpec((1, 3, NA_QN, NA_KN), lambda b, h: (h, 0, 0, 0))],
        out_specs=pl.BlockSpec((S, NA_HEAD), lambda b, h: (b, h)),
        out_shape=jax.ShapeDtypeStruct((B * S, NA_WIDTH), BF16),
        compiler_params=_cparams(("parallel", "arbitrary")),
        name="na_window",
    )(p_na, p_na, p_na, p_na, p_na, bias)
    y_meta = pl.pallas_call(
        _na_meta_kernel,
        grid=(B, NA_HEADS),
        in_specs=[pl.BlockSpec((META_BLK, NA_HEAD), lambda b, h: (meta_blk + b, h)),
                  pl.BlockSpec((META_BLK, NA_HEAD), lambda b, h: (meta_blk + b, NA_HEADS + h)),
                  pl.BlockSpec((META_BLK, NA_HEAD), lambda b, h: (meta_blk + b, 2 * NA_HEADS + h))],
        out_specs=pl.BlockSpec((META_BLK, NA_HEAD), lambda b, h: (b, h)),
        out_shape=jax.ShapeDtypeStruct((B * META_BLK, NA_WIDTH), BF16),
        compiler_params=_cparams(("parallel", "arbitrary")),
        name="na_meta",
    )(p_na, p_na, p_na)
    return jnp.concatenate([y, y_meta], axis=0)


def _pair_sum_matrix(value):
    r = np.arange(LANE)[:, None] // RW_HEAD
    c = np.arange(LANE)[None, :] // RW_HEAD
    return jnp.asarray(np.where(r == c, value, 0.0), F32)


def _rw_prep_kernel(x_ref, pb_ref, nb_ref, mup_ref, mun_ref, w0_ref, wup_ref, a0_ref, aup_ref, gup_ref,
                    kk_w_ref, ka_ref, ones_ref,
                    r_ref, v_ref, kk_ref, lw_ref, kd_ref, bd_ref, g_ref, *, n_real_tiles, tiles_per_batch):
    i = pl.program_id(0)
    tl = x_ref.shape[0]
    C = RW_WIDTH
    is_meta = i >= n_real_tiles
    last = (i % tiles_per_batch) == tiles_per_batch - 1
    x = x_ref[...]
    rid = lax.broadcasted_iota(jnp.int32, (tl, 1), 0)
    prev = jnp.where(rid == 0, pb_ref[7:8, :], pltpu.roll(x, 1, axis=0))
    nxt = jnp.where(rid == tl - 1, nb_ref[0:1, :], pltpu.roll(x, tl - 1, axis=0))
    no_prev_upto = jnp.where(is_meta, META_OFF, -1)
    no_next_from = jnp.where(jnp.logical_and(jnp.logical_not(is_meta), last), tl - 1, tl)
    prev = jnp.where(rid <= no_prev_upto, 0.0, prev)
    nxt = jnp.where(rid >= no_next_from, 0.0, nxt)
    xs = x + mup_ref[...] * (prev - x) + mun_ref[...] * (nxt - x)

    live = rid >= jnp.where(is_meta, META_OFF, 0)
    r = xs[:, 0:C]
    k = xs[:, C:2 * C]
    v = xs[:, 2 * C:3 * C]
    o = 3 * C
    w_dn = xs[:, o:o + 2 * DECAY_LORA]
    a_dn = xs[:, o + 2 * DECAY_LORA:o + 2 * DECAY_LORA + 2 * ICL_LORA]
    g_dn = xs[:, o + 2 * DECAY_LORA + 2 * ICL_LORA:]
    w_log = w0_ref[...] + _dot(jnp.tanh(w_dn).astype(BF16), wup_ref[...])
    a = jax.nn.sigmoid(a0_ref[...] + _dot(a_dn.astype(BF16), aup_ref[...]))
    g_ref[...] = _dot(jax.nn.sigmoid(g_dn).astype(BF16), gup_ref[...])
    logw = -math.exp(-0.5) * jax.nn.sigmoid(w_log)
    kk_raw = k * kk_w_ref[...]
    ka = ka_ref[...]
    ones = ones_ref[...]
    for hp in range(RW_PAIRS):
        sl = slice(hp * LANE, (hp + 1) * LANE)
        kr = kk_raw[:, sl]
        nrm = jnp.sqrt(_dot(kr * kr, ones, HIGHEST))
        kk = jnp.where(live, kr / jnp.maximum(nrm, 1e-12), 0.0)
        r_ref[hp] = jnp.where(live, r[:, sl], 0.0)
        v_ref[hp] = jnp.where(live, v[:, sl], 0.0)
        kk_ref[hp] = kk
        kp = k[:, sl]
        for d in range(2):
            sld = slice(d * C + hp * LANE, d * C + (hp + 1) * LANE)
            ad = a[:, sld]
            lw_ref[d, hp] = jnp.where(live, logw[:, sld], 0.0)
            kd_ref[d, hp] = jnp.where(live, kp * (1.0 + (ad - 1.0) * ka[:, sl]), 0.0)
            bd_ref[d, hp] = kk * ad


def _rw_prep(p_rw, prm, B, S):
    T = p_rw.shape[0]
    tl = PREP_ROWS
    n_real_tiles = B * S // tl
    tiles_per_batch = S // tl
    sub = tl // 8

    def prev_map(i):
        b = i // tiles_per_batch
        meta_last = (B * S + META_BLK * b + META_BLK - 8) // 8
        real = jnp.where(i % tiles_per_batch == 0, meta_last, i * sub - 1)
        return (jnp.where(i < n_real_tiles, real, jnp.maximum(i * sub - 1, 0)), 0)

    def next_map(i):
        real = jnp.where(i % tiles_per_batch == tiles_per_batch - 1, 0, (i + 1) * sub)
        return (jnp.where(i < n_real_tiles, real, (i - n_real_tiles) * (S // 8)), 0)

    fix = lambda i: (0, 0)
    pair = pl.BlockSpec((RW_PAIRS, tl, LANE), lambda i: (0, i, 0))
    pair2 = pl.BlockSpec((2, RW_PAIRS, tl, LANE), lambda i: (0, 0, i, 0))
    pair_shape = jax.ShapeDtypeStruct((RW_PAIRS, T, LANE), F32)
    pair2_shape = jax.ShapeDtypeStruct((2, RW_PAIRS, T, LANE), F32)
    C = RW_WIDTH
    return pl.pallas_call(
        functools.partial(_rw_prep_kernel, n_real_tiles=n_real_tiles, tiles_per_batch=tiles_per_batch),
        grid=(T // tl,),
        in_specs=[pl.BlockSpec((tl, RW_PROJ_PAD), lambda i: (i, 0)),
                  pl.BlockSpec((8, RW_PROJ_PAD), prev_map),
                  pl.BlockSpec((8, RW_PROJ_PAD), next_map),
                  pl.BlockSpec((1, RW_PROJ_PAD), fix),
                  pl.BlockSpec((1, RW_PROJ_PAD), fix),
                  pl.BlockSpec((1, 2 * C), fix),
                  pl.BlockSpec((2 * DECAY_LORA, 2 * C), fix),
                  pl.BlockSpec((1, 2 * C), fix),
                  pl.BlockSpec((2 * ICL_LORA, 2 * C), fix),
                  pl.BlockSpec((GATE_PAD, C), fix),
                  pl.BlockSpec((1, C), fix),
                  pl.BlockSpec((1, C), fix),
                  pl.BlockSpec((LANE, LANE), fix)],
        out_specs=[pair, pair, pair, pair2, pair2, pair2, pl.BlockSpec((tl, C), lambda i: (i, 0))],
        out_shape=[pair_shape, pair_shape, pair_shape, pair2_shape, pair2_shape, pair2_shape,
                   jax.ShapeDtypeStruct((T, C), F32)],
        compiler_params=_cparams(("parallel",)),
        name="rwkv_prep",
    )(p_rw, p_rw, p_rw, prm["mu_prev"], prm["mu_next"], prm["w0"], prm["w_up"], prm["a0"], prm["a_up"],
      prm["g_up"], prm["k_k"], prm["k_a"], _pair_sum_matrix(1.0))


SCAN_GROUP = 12


def _rw_scan_kernel(r_ref, v_ref, kk_ref, lw_ref, kd_ref, bd_ref, o_ref, s_sc):
    d = pl.program_id(0) % 2
    c = pl.program_id(1)

    @pl.when(c == 0)
    def _():
        s_sc[...] = jnp.zeros(s_sc.shape, F32)

    P2 = 2 * CHUNK
    sign = 1 - 2 * d
    row = lax.broadcasted_iota(jnp.int32, (CHUNK, CHUNK), 0)
    col = lax.broadcasted_iota(jnp.int32, (CHUNK, CHUNK), 1)
    tri = jnp.where((row - col) * sign >= 0, 1.0, 0.0)
    ri = lax.broadcasted_iota(jnp.int32, (P2, P2), 0)
    ci = lax.broadcasted_iota(jnp.int32, (P2, P2), 1)
    same = (ri // CHUNK) == (ci // CHUNK)
    lag = (ri % CHUNK - ci % CHUNK) * sign
    strict = jnp.logical_and(same, lag > 0)
    eye = jnp.where(ri == ci, 1.0, 0.0)
    ri2 = lax.broadcasted_iota(jnp.int32, (P2, 2 * P2), 0)
    ci2 = lax.broadcasted_iota(jnp.int32, (P2, 2 * P2), 1) % P2
    incl2 = jnp.logical_and((ri2 // CHUNK) == (ci2 // CHUNK), (ri2 % CHUNK - ci2 % CHUNK) * sign >= 0)
    head0 = lax.broadcasted_iota(jnp.int32, (CHUNK, P2), 1) < RW_HEAD
    n_levels = int(math.log2(CHUNK)) - 1

    def stack(z):
        return jnp.concatenate([jnp.where(head0, z, 0.0), jnp.where(head0, 0.0, z)], axis=0)

    def prep(hp):
        lw = lw_ref[0, hp]
        cum = _dot(tri, lw, HIGHEST)
        p_inv = jnp.exp(-cum)
        x = jnp.concatenate([stack(-kk_ref[hp] * jnp.exp(cum - lw)),
                             stack(r_ref[hp] * jnp.exp(cum))], axis=0).astype(BF16)
        y = jnp.concatenate([stack(bd_ref[0, hp] * p_inv),
                             stack(kd_ref[0, hp] * p_inv)], axis=0).astype(BF16)
        vbd = stack(v_ref[hp]).astype(BF16)
        p_tot = jnp.exp(jnp.sum(lw, axis=0, keepdims=True))
        return x, y, vbd, p_tot

    for g0 in range(0, RW_PAIRS, SCAN_GROUP):
        hps = list(range(g0, g0 + SCAN_GROUP))
        pre = [prep(hp) for hp in hps]
        xs_ = [p[0] for p in pre]
        ys_ = [p[1] for p in pre]
        vs_ = [p[2] for p in pre]
        s_old = [s_sc[hp] for hp in hps]
        gram = [_dot_nt(x, y) for x, y in zip(xs_, ys_)]
        xst = [_dot_nt(x, s.astype(BF16)) for x, s in zip(xs_, s_old)]
        a_ab = [jnp.where(strict, g[:P2, :P2], 0.0) for g in gram]
        t_inv = [eye + a for a in a_ab]
        pw = [a.astype(BF16) for a in a_ab]
        for _ in range(n_levels):
            pw = [_dot(p, p).astype(BF16) for p in pw]
            t_inv = [t + _dot(t.astype(BF16), p) for t, p in zip(t_inv, pw)]
        rhs = [xt[:P2] + _dot(jnp.where(strict, g[:P2, P2:], 0.0).astype(BF16), vb)
               for xt, g, vb in zip(xst, gram, vs_)]
        u = [_dot(t.astype(BF16), r_.astype(BF16)) for t, r_ in zip(t_inv, rhs)]
        uv = [jnp.concatenate([u_.astype(BF16), vb], axis=0) for u_, vb in zip(u, vs_)]
        for i, hp in enumerate(hps):
            a_r = jnp.where(incl2, gram[i][P2:, :], 0.0).astype(BF16)
            o_bd = xst[i][P2:] + _dot(a_r, uv[i])
            o_ref[0, hp] = o_bd[:CHUNK] + o_bd[CHUNK:]
            s_sc[hp] = (s_old[i] + _dot_tn(uv[i], ys_[i])) * pre[i][3]


def _rw_scan(r, v, kk, lw, kd, bd, B, S):
    T = r.shape[1]
    nc = S // CHUNK + 1
    meta_chunk = (B * S + META_BLK - CHUNK) // CHUNK

    def blk(g, c):
        b = g // 2
        j = jnp.where(g % 2 == 0, c, nc - 1 - c)
        return jnp.where(j == 0, meta_chunk + b * (META_BLK // CHUNK), b * (S // CHUNK) + j - 1)

    shared = pl.BlockSpec((RW_PAIRS, CHUNK, LANE), lambda g, c: (0, blk(g, c), 0))
    per_dir = pl.BlockSpec((1, RW_PAIRS, CHUNK, LANE), lambda g, c: (g % 2, 0, blk(g, c), 0))
    return pl.pallas_call(
        _rw_scan_kernel,
        grid=(2 * B, nc),
        in_specs=[shared, shared, shared, per_dir, per_dir, per_dir],
        out_specs=per_dir,
        out_shape=jax.ShapeDtypeStruct((2, RW_PAIRS, T, LANE), F32),
        scratch_shapes=[pltpu.VMEM((RW_PAIRS, LANE, LANE), F32)],
        compiler_params=_cparams(("parallel", "arbitrary")),
        name="rwkv_scan",
    )(r, v, kk, lw, kd, bd)


def _rw_post_kernel(o_ref, r_ref, v_ref, kd_ref, g_ref, rk_ref, gw_ref, gb_ref, mean_ref, ones_ref, y_ref, *,
                    n_real_tiles):
    mean_m = mean_ref[...]
    ones = ones_ref[...]
    tl = y_ref.shape[0]
    rid = lax.broadcasted_iota(jnp.int32, (tl, 1), 0) % META_BLK
    live = rid >= jnp.where(pl.program_id(0) >= n_real_tiles, META_OFF, 0)
    for hp in range(RW_PAIRS):
        sl = slice(hp * LANE, (hp + 1) * LANE)
        y = o_ref[0, hp] + o_ref[1, hp]
        yc = y - _dot(y, mean_m, HIGHEST)
        var = _dot(yc * yc, mean_m, HIGHEST)
        yn = yc * lax.rsqrt(var + GN_EPS) * gw_ref[:, sl] + gb_ref[:, sl]
        k_bonus = 0.5 * (kd_ref[0, hp] + kd_ref[1, hp])
        bonus = _dot(r_ref[hp] * k_bonus * rk_ref[:, sl], ones, HIGHEST) * v_ref[hp]
        y_ref[:, sl] = jnp.where(live, (yn + bonus) * g_ref[:, sl], 0.0).astype(y_ref.dtype)


def _rw_post(o, r, v, kd, g, prm, n_real):
    T = g.shape[0]
    tl = _row_tile(T, 256)
    assert n_real % tl == 0
    C = RW_WIDTH
    fix = lambda i: (0, 0)
    pair = pl.BlockSpec((RW_PAIRS, tl, LANE), lambda i: (0, i, 0))
    pair2 = pl.BlockSpec((2, RW_PAIRS, tl, LANE), lambda i: (0, 0, i, 0))
    return pl.pallas_call(
        functools.partial(_rw_post_kernel, n_real_tiles=n_real // tl),
        grid=(T // tl,),
        in_specs=[pair2, pair, pair, pair2,
                  pl.BlockSpec((tl, C), lambda i: (i, 0)),
                  pl.BlockSpec((1, C), fix), pl.BlockSpec((1, C), fix), pl.BlockSpec((1, C), fix),
                  pl.BlockSpec((LANE, LANE), fix), pl.BlockSpec((LANE, LANE), fix)],
        out_specs=pl.BlockSpec((tl, C), lambda i: (i, 0)),
        out_shape=jax.ShapeDtypeStruct((T, C), BF16),
        compiler_params=_cparams(("parallel",)),
        name="rwkv_post",
    )(o, r, v, kd, g, prm["r_k"], prm["gn_w"], prm["gn_b"],
      _pair_sum_matrix(1.0 / RW_HEAD), _pair_sum_matrix(1.0))


def _block_diag2(m):
    z = jnp.zeros_like(m[0])
    return jnp.concatenate([jnp.concatenate([m[0], z], axis=1), jnp.concatenate([z, m[1]], axis=1)], axis=0)


def _rwkv7(p_rw, mu_prev, mu_next, w0, w_up, a0, a_up, g_up, k_k, k_a, r_k, gn_w, gn_b, B, S):
    C = RW_WIDTH
    pad = RW_PROJ_PAD - RW_PROJ
    prm = {
        "mu_prev": jnp.pad(mu_prev, (0, pad)).reshape(1, RW_PROJ_PAD),
        "mu_next": jnp.pad(mu_next, (0, pad)).reshape(1, RW_PROJ_PAD),
        "w0": w0.reshape(1, 2 * C),
        "w_up": _block_diag2(w_up).astype(BF16),
        "a0": a0.reshape(1, 2 * C),
        "a_up": _block_diag2(a_up).astype(BF16),
        "g_up": jnp.pad(g_up, ((0, GATE_PAD - GATE_LORA), (0, 0))).astype(BF16),
        "k_k": k_k.reshape(1, C),
        "k_a": k_a.reshape(1, C),
        "r_k": r_k.reshape(1, C),
        "gn_w": gn_w.reshape(1, C),
        "gn_b": gn_b.reshape(1, C),
    }
    r, v, kk, lw, kd, bd, g = _rw_prep(p_rw, prm, B, S)
    o = _rw_scan(r, v, kk, lw, kd, bd, B, S)
    return _rw_post(o, r, v, kd, g, prm, B * S)


def kernel(x, meta_tokens, ffn1_w_in, ffn1_w_out, ln_ffn1_g, ln_ffn1_b, w_in, rw_mu_prev, rw_mu_next, rw_w0, rw_w_up, rw_a0, rw_a_up, rw_g_up, rw_k_k, rw_k_a, rw_r_k, rw_gn_w, rw_gn_b, na_rpb, gq_q_gain, gq_k_gain, w_out, ln_mix_g, ln_mix_b, ffn2_w_in, ffn2_w_out, ln_ffn2_g, ln_ffn2_b):
    B, S, D = x.shape
    assert D == D_MODEL and S % 1024 == 0
    meta = jnp.zeros((B, META_BLK, D), x.dtype).at[:, META_OFF:].set(
        jnp.broadcast_to(meta_tokens[None].astype(x.dtype), (B, N_META, D)))
    h = jnp.concatenate([x.reshape(B * S, D), meta.reshape(B * META_BLK, D)], axis=0)
    hb = h.astype(BF16)
    cos_t, sin_t = _rope_tables(B, S)
    for l in range(DEPTH):
        h, hb = _ffn(h, hb, ffn1_w_in[l], ffn1_w_out[l], ln_ffn1_g[l], ln_ffn1_b[l])
        w = w_in[l]
        w_rw = jnp.pad(w[:, :RW_PROJ], ((0, 0), (0, RW_PROJ_PAD - RW_PROJ))).astype(BF16)
        w_na = w[:, RW_PROJ:RW_PROJ + NA_PROJ].astype(BF16)
        w_gq = w[:, RW_PROJ + NA_PROJ:].astype(BF16)
        p_rw = _matmul(hb, w_rw, F32)
        p_na = _matmul(hb, w_na, BF16)
        p_gq = _matmul(hb, w_gq, F32)
        y_a = _rwkv7(p_rw, rw_mu_prev[l], rw_mu_next[l], rw_w0[l], rw_w_up[l], rw_a0[l], rw_a_up[l],
                     rw_g_up[l], rw_k_k[l], rw_k_a[l], rw_r_k[l], rw_gn_w[l], rw_gn_b[l], B, S)
        y_b = _neighborhood_attention(p_na, na_rpb[l], B, S)
        q, k, v = _gq_prep(p_gq, cos_t, sin_t, gq_q_gain[l], gq_k_gain[l])
        y_c = _gqa(q, k, v, B, S)
        z = _mix_out_matmul(y_a, y_b, y_c, w_out[l].astype(BF16), h)
        h, hb = _layer_norm(z, ln_mix_g[l], ln_mix_b[l])
        h, hb = _ffn(h, hb, ffn2_w_in[l], ffn2_w_out[l], ln_ffn2_g[l], ln_ffn2_b[l])
    return h[:B * S].reshape(B, S, D)
```

```python
import functools
import math

import numpy as np
import jax
import jax.numpy as jnp
from jax import lax
from jax.experimental import pallas as pl
from jax.experimental.pallas import tpu as pltpu

F32 = jnp.float32
BF16 = jnp.bfloat16
HIGHEST = lax.Precision.HIGHEST

D_MODEL = 4096
DEPTH = 4
N_META = 16
GRID_W = 64
RW_HEAD = 64
RW_WIDTH = 1536
RW_HEADS = RW_WIDTH // RW_HEAD
RW_PAIRS = RW_HEADS // 2
DECAY_LORA = 64
ICL_LORA = 64
GATE_LORA = 224
RW_PROJ = 3 * RW_WIDTH + 2 * DECAY_LORA + 2 * ICL_LORA + GATE_LORA
GN_EPS = 64e-5
NA_HEAD = 128
NA_WIDTH = 1536
NA_HEADS = NA_WIDTH // NA_HEAD
NA_KR = 8
NA_KC = 16
NA_PROJ = 3 * NA_WIDTH
GQ_HEAD = 128
GQ_WIDTH = 1024
GQ_HEADS = GQ_WIDTH // GQ_HEAD
GQ_KV_HEADS = 2
GQ_GROUP = GQ_HEADS // GQ_KV_HEADS
GQ_KV_W = GQ_KV_HEADS * GQ_HEAD
GQ_PROJ = GQ_WIDTH + 2 * GQ_KV_W
ROPE_THETA = 10000.0
LOG2E = math.log2(math.e)
QK_EPS = 1e-6
D_FF = 3 * D_MODEL // 2
LN_EPS = 1e-5
ALPHA = (2.0 * DEPTH) ** 0.25

LANE = 128
META_BLK = 128
META_OFF = META_BLK - N_META
RW_PROJ_PAD = 5120
GATE_PAD = 256
CHUNK = 64
PREP_ROWS = 128
GQ_KV_TILE = 8192
NEG = -1e30
VMEM_LIMIT = 56 * 2 ** 20


def _cparams(sem):
    return pltpu.CompilerParams(dimension_semantics=sem, vmem_limit_bytes=VMEM_LIMIT)


def _row_tile(total, cap):
    best = LANE
    for t in range(LANE, cap + 1, LANE):
        if total % t == 0:
            best = t
    return best


def _dot_nt(a, b, precision=None):
    return lax.dot_general(a, b, (((1,), (1,)), ((), ())), preferred_element_type=F32, precision=precision)


def _dot_tn(a, b, precision=None):
    return lax.dot_general(a, b, (((0,), (0,)), ((), ())), preferred_element_type=F32, precision=precision)


def _dot(a, b, precision=None):
    return jnp.dot(a, b, preferred_element_type=F32, precision=precision)


def _bf16_terms(a, n):
    terms = []
    for _ in range(n - 1):
        t = a.astype(BF16)
        terms.append(t)
        a = a - t.astype(F32)
    terms.append(a.astype(BF16))
    return terms


def _dot_split_lhs(a, m_bf16, n=3):
    return sum(_dot(t, m_bf16) for t in _bf16_terms(a, n))


def _mm_kernel(x_ref, w_ref, o_ref):
    o_ref[...] = _dot(x_ref[...], w_ref[...]).astype(o_ref.dtype)


def _matmul(x, w, out_dtype, tn=512):
    T, K = x.shape
    N = w.shape[1]
    tm = _row_tile(T, 1280)
    return pl.pallas_call(
        _mm_kernel,
        grid=(T // tm, N // tn),
        in_specs=[pl.BlockSpec((tm, K), lambda i, j: (i, 0)),
                  pl.BlockSpec((K, tn), lambda i, j: (0, j))],
        out_specs=pl.BlockSpec((tm, tn), lambda i, j: (i, j)),
        out_shape=jax.ShapeDtypeStruct((T, N), out_dtype),
        compiler_params=_cparams(("parallel", "arbitrary")),
        name="proj_matmul",
    )(x, w)


def _swiglu_kernel(x_ref, wg_ref, wu_ref, o_ref):
    x = x_ref[...]
    g = _dot(x, wg_ref[...])
    u = _dot(x, wu_ref[...])
    o_ref[...] = (g * jax.nn.sigmoid(g) * u).astype(o_ref.dtype)


def _swiglu_matmul(x, w_in, tn=512):
    T, K = x.shape
    F = w_in.shape[1] // 2
    tm = _row_tile(T, 1280)
    nj = F // tn
    return pl.pallas_call(
        _swiglu_kernel,
        grid=(T // tm, nj),
        in_specs=[pl.BlockSpec((tm, K), lambda i, j: (i, 0)),
                  pl.BlockSpec((K, tn), lambda i, j: (0, j)),
                  pl.BlockSpec((K, tn), lambda i, j: (0, j + nj))],
        out_specs=pl.BlockSpec((tm, tn), lambda i, j: (i, j)),
        out_shape=jax.ShapeDtypeStruct((T, F), BF16),
        compiler_params=_cparams(("parallel", "arbitrary")),
        name="ffn_swiglu",
    )(x, w_in, w_in)


def _mm_resid_kernel(x_ref, w_ref, h_ref, o_ref, *, scale):
    o_ref[...] = ALPHA * h_ref[...] + scale * _dot(x_ref[...], w_ref[...])


def _matmul_resid(x, w, h, scale, tn=512):
    T, K = x.shape
    N = w.shape[1]
    tm = _row_tile(T, 640)
    return pl.pallas_call(
        functools.partial(_mm_resid_kernel, scale=scale),
        grid=(T // tm, N // tn),
        in_specs=[pl.BlockSpec((tm, K), lambda i, j: (i, 0)),
                  pl.BlockSpec((K, tn), lambda i, j: (0, j)),
                  pl.BlockSpec((tm, tn), lambda i, j: (i, j))],
        out_specs=pl.BlockSpec((tm, tn), lambda i, j: (i, j)),
        out_shape=jax.ShapeDtypeStruct((T, N), F32),
        compiler_params=_cparams(("parallel", "arbitrary")),
        name="resid_matmul",
    )(x, w, h)


def _mix_out_kernel(ya_ref, yb_ref, yc_ref, wa_ref, wb_ref, wc_ref, h_ref, o_ref):
    mix = _dot(ya_ref[...], wa_ref[...]) + _dot(yb_ref[...], wb_ref[...]) + _dot(yc_ref[...], wc_ref[...])
    o_ref[...] = ALPHA * h_ref[...] + mix


def _mix_out_matmul(y_a, y_b, y_c, w, h, tn=512):
    T = h.shape[0]
    N = w.shape[1]
    assert RW_WIDTH == NA_WIDTH and (RW_WIDTH + NA_WIDTH) % GQ_WIDTH == 0
    tm = _row_tile(T, 640)
    row = lambda i, j: (i, 0)
    return pl.pallas_call(
        _mix_out_kernel,
        grid=(T // tm, N // tn),
        in_specs=[pl.BlockSpec((tm, RW_WIDTH), row),
                  pl.BlockSpec((tm, NA_WIDTH), row),
                  pl.BlockSpec((tm, GQ_WIDTH), row),
                  pl.BlockSpec((RW_WIDTH, tn), lambda i, j: (0, j)),
                  pl.BlockSpec((NA_WIDTH, tn), lambda i, j: (1, j)),
                  pl.BlockSpec((GQ_WIDTH, tn), lambda i, j: ((RW_WIDTH + NA_WIDTH) // GQ_WIDTH, j)),
                  pl.BlockSpec((tm, tn), lambda i, j: (i, j))],
        out_specs=pl.BlockSpec((tm, tn), lambda i, j: (i, j)),
        out_shape=jax.ShapeDtypeStruct((T, N), F32),
        compiler_params=_cparams(("parallel", "arbitrary")),
        name="mix_out_matmul",
    )(y_a, y_b, y_c, w, w, w, h)


def _ln_kernel(z_ref, g_ref, b_ref, h_ref, hb_ref):
    z = z_ref[...]
    mu = jnp.mean(z, axis=-1, keepdims=True)
    zc = z - mu
    var = jnp.mean(zc * zc, axis=-1, keepdims=True)
    y = zc * lax.rsqrt(var + LN_EPS) * g_ref[...] + b_ref[...]
    h_ref[...] = y
    hb_ref[...] = y.astype(BF16)


def _layer_norm(z, g, b):
    T, D = z.shape
    tm = _row_tile(T, 256)
    return pl.pallas_call(
        _ln_kernel,
        grid=(T // tm,),
        in_specs=[pl.BlockSpec((tm, D), lambda i: (i, 0)),
                  pl.BlockSpec((1, D), lambda i: (0, 0)),
                  pl.BlockSpec((1, D), lambda i: (0, 0))],
        out_specs=[pl.BlockSpec((tm, D), lambda i: (i, 0)),
                   pl.BlockSpec((tm, D), lambda i: (i, 0))],
        out_shape=[jax.ShapeDtypeStruct((T, D), F32), jax.ShapeDtypeStruct((T, D), BF16)],
        compiler_params=_cparams(("parallel",)),
        name="layer_norm",
    )(z, g.reshape(1, D), b.reshape(1, D))


def _ffn(h, hb, w_in, w_out, g, b):
    act = _swiglu_matmul(hb, w_in.astype(BF16))
    z = _matmul_resid(act, w_out.astype(BF16), h, 0.5)
    return _layer_norm(z, g, b)


def _rope_tables(B, S):
    t = jnp.arange(S)
    row = (t // GRID_W).astype(F32)
    col = (t % GRID_W).astype(F32)
    half = GQ_HEAD // 4
    inv = ROPE_THETA ** (-jnp.arange(half, dtype=F32) / half)
    ang = jnp.concatenate([row[:, None] * inv, col[:, None] * inv], axis=-1)
    cos, sin = jnp.cos(ang), jnp.sin(ang)
    cos2 = jnp.concatenate([cos, cos], axis=-1)
    sin2 = jnp.concatenate([-sin, sin], axis=-1)
    pad_c = jnp.ones((B * META_BLK, GQ_HEAD), F32)
    pad_s = jnp.zeros((B * META_BLK, GQ_HEAD), F32)
    return (jnp.concatenate([jnp.tile(cos2, (B, 1)), pad_c], axis=0),
            jnp.concatenate([jnp.tile(sin2, (B, 1)), pad_s], axis=0))


def _gq_prep_kernel(p_ref, cos_ref, sin_ref, qg_ref, kg_ref, q_ref, k_ref, v_ref):
    cos = cos_ref[...]
    sin = sin_ref[...]

    def rms_rope(x, gain):
        x = x * lax.rsqrt(jnp.mean(x * x, axis=-1, keepdims=True) + QK_EPS) * gain
        return x * cos + pltpu.roll(x, GQ_HEAD // 2, axis=1) * sin

    for hd in range(GQ_HEADS):
        sl = slice(hd * GQ_HEAD, (hd + 1) * GQ_HEAD)
        q_ref[:, sl] = (rms_rope(p_ref[:, sl], qg_ref[...]) * (GQ_HEAD ** -0.5 * LOG2E)).astype(BF16)
    for hd in range(GQ_KV_HEADS):
        sl = slice(hd * GQ_HEAD, (hd + 1) * GQ_HEAD)
        slk = slice(GQ_WIDTH + hd * GQ_HEAD, GQ_WIDTH + (hd + 1) * GQ_HEAD)
        slv = slice(GQ_WIDTH + GQ_KV_W + hd * GQ_HEAD, GQ_WIDTH + GQ_KV_W + (hd + 1) * GQ_HEAD)
        k_ref[:, sl] = rms_rope(p_ref[:, slk], kg_ref[...]).astype(BF16)
        v_ref[:, sl] = p_ref[:, slv].astype(BF16)


def _gq_prep(p_gq, cos_t, sin_t, q_gain, k_gain):
    T = p_gq.shape[0]
    tm = _row_tile(T, 256)
    row = lambda i: (i, 0)
    fix = lambda i: (0, 0)
    return pl.pallas_call(
        _gq_prep_kernel,
        grid=(T // tm,),
        in_specs=[pl.BlockSpec((tm, GQ_PROJ), row),
                  pl.BlockSpec((tm, GQ_HEAD), row),
                  pl.BlockSpec((tm, GQ_HEAD), row),
                  pl.BlockSpec((1, GQ_HEAD), fix),
                  pl.BlockSpec((1, GQ_HEAD), fix)],
        out_specs=[pl.BlockSpec((tm, GQ_WIDTH), row),
                   pl.BlockSpec((tm, GQ_KV_W), row),
                   pl.BlockSpec((tm, GQ_KV_W), row)],
        out_shape=[jax.ShapeDtypeStruct((T, GQ_WIDTH), BF16),
                   jax.ShapeDtypeStruct((T, GQ_KV_W), BF16),
                   jax.ShapeDtypeStruct((T, GQ_KV_W), BF16)],
        compiler_params=_cparams(("parallel",)),
        name="gqa_prep",
    )(p_gq, cos_t, sin_t, q_gain.reshape(1, GQ_HEAD), k_gain.reshape(1, GQ_HEAD))


def _flash_kernel(q_ref, k_ref, v_ref, km_ref, vm_ref, o_ref, m_sc, acc_sc, *, nk):
    ki = pl.program_id(3)
    qs = [q_ref[:, g * GQ_HEAD:(g + 1) * GQ_HEAD] for g in range(GQ_GROUP)]

    def attend(k, v, mask):
        v_ones = jnp.concatenate([v, jnp.ones(v.shape, BF16)], axis=1)

        def scores(g):
            s = _dot_nt(qs[g], k)
            return s if mask is None else jnp.where(mask, s, NEG)

        def update(g, s):
            m_prev = m_sc[g]
            m_new = jnp.maximum(m_prev, jnp.max(s, axis=-1, keepdims=True))
            p = jnp.exp2(s - m_new).astype(BF16)
            acc_sc[g] = jnp.exp2(m_prev - m_new) * acc_sc[g] + _dot(p, v_ones)
            m_sc[g] = m_new

        s_next = scores(0)
        for g in range(GQ_GROUP):
            s_cur = s_next
            if g + 1 < GQ_GROUP:
                s_next = scores(g + 1)
            update(g, s_cur)

    @pl.when(ki == 0)
    def _():
        m_sc[...] = jnp.full(m_sc.shape, NEG, F32)
        acc_sc[...] = jnp.zeros(acc_sc.shape, F32)
        is_meta = lax.broadcasted_iota(jnp.int32, (1, META_BLK), 1) >= META_OFF
        attend(km_ref[...], vm_ref[...], is_meta)

    attend(k_ref[...], v_ref[...], None)

    @pl.when(ki == nk - 1)
    def _():
        for g in range(GQ_GROUP):
            o = acc_sc[g, :, :GQ_HEAD] / acc_sc[g, :, GQ_HEAD:]
            o_ref[:, g * GQ_HEAD:(g + 1) * GQ_HEAD] = o.astype(o_ref.dtype)


def _gqa(q, k, v, B, S):
    T = q.shape[0]
    tq = 128
    tk = _row_tile(S, GQ_KV_TILE)
    nq_real = S // tq
    nq = nq_real + META_BLK // tq
    nk = S // tk
    meta_q = B * S // tq
    meta_k = B * S // META_BLK

    def q_map(b, g, qi, ki):
        return (jnp.where(qi < nq_real, b * nq_real + qi, meta_q + b * (META_BLK // tq) + (qi - nq_real)), g)

    return pl.pallas_call(
        functools.partial(_flash_kernel, nk=nk),
        grid=(B, GQ_KV_HEADS, nq, nk),
        in_specs=[pl.BlockSpec((tq, GQ_GROUP * GQ_HEAD), q_map),
                  pl.BlockSpec((tk, GQ_HEAD), lambda b, g, qi, ki: (b * nk + ki, g)),
                  pl.BlockSpec((tk, GQ_HEAD), lambda b, g, qi, ki: (b * nk + ki, g)),
                  pl.BlockSpec((META_BLK, GQ_HEAD), lambda b, g, qi, ki: (meta_k + b, g)),
                  pl.BlockSpec((META_BLK, GQ_HEAD), lambda b, g, qi, ki: (meta_k + b, g))],
        out_specs=pl.BlockSpec((tq, GQ_GROUP * GQ_HEAD), q_map),
        out_shape=jax.ShapeDtypeStruct((T, GQ_WIDTH), BF16),
        scratch_shapes=[pltpu.VMEM((GQ_GROUP, tq, 1), F32),
                        pltpu.VMEM((GQ_GROUP, tq, 2 * GQ_HEAD), F32)],
        compiler_params=_cparams(("parallel", "parallel", "parallel", "arbitrary")),
        name="gqa_flash",
    )(q, k, v, k, v)


NA_QROWS = 8
NA_KROWS = 16
NA_QN = NA_QROWS * GRID_W
NA_KN = NA_KROWS * GRID_W


def _na_bias_tables(rpb):
    qi = np.arange(NA_QROWS)[:, None]
    kr = np.arange(NA_KROWS)[None, :]
    qj = np.arange(GRID_W)[:, None]
    kc = np.arange(GRID_W)[None, :]
    c0 = np.clip(qj - NA_KC // 2, 0, GRID_W - NA_KC)
    valid_c = (kc >= c0) & (kc < c0 + NA_KC)
    col_off = kc - qj + NA_KC - 1
    n_dr, n_dc = 2 * NA_KR - 1, 2 * NA_KC - 1
    col_sel = (col_off[None] == np.arange(n_dc)[:, None, None]) & valid_c[None]
    row_sel, row_ok = [], []
    for i_rel, r0_rel in ((qi, np.maximum(qi - NA_KR // 2, 0)),
                          (qi + NA_KR // 2, qi),
                          (qi + NA_KR, np.minimum(qi + NA_KR // 2, NA_KR))):
        valid_r = (kr >= r0_rel) & (kr < r0_rel + NA_KR)
        row_off = kr - i_rel + NA_KR - 1
        row_sel.append((row_off[..., None] == np.arange(n_dr)) & valid_r[..., None])
        row_ok.append(valid_r)
    row_sel = np.stack(row_sel).astype(np.float32)
    valid = np.stack(row_ok)[:, :, None, :, None] & valid_c[None, None, :, None, :]
    band = jnp.einsum("hrd,dqk->hrqk", rpb, col_sel.astype(np.float32), precision=HIGHEST)
    tab = jnp.einsum("vior,hrqk->hviqok", row_sel, band, precision=HIGHEST)
    tab = jnp.where(valid[None], tab, NEG)
    return tab.reshape(NA_HEADS, 3, NA_QN, NA_KN)


def _na_kernel(q_ref, k_ref, v_ref, km_ref, vm_ref, bias_ref, o_ref, *, rows):
    nblk = rows // NA_QROWS
    scale = NA_HEAD ** -0.5
    km = km_ref[...]
    vm = vm_ref[...]
    is_meta = lax.broadcasted_iota(jnp.int32, (1, META_BLK), 1) >= META_OFF

    def body(blk, carry):
        q = q_ref[pl.ds(pl.multiple_of(blk * NA_QN, NA_QN), NA_QN), :]
        krow = jnp.clip(blk * NA_QROWS - NA_KR // 2, 0, rows - NA_KROWS)
        ks = pl.multiple_of(krow * GRID_W, 4 * GRID_W)
        kw = k_ref[pl.ds(ks, NA_KN), :]
        vw = v_ref[pl.ds(ks, NA_KN), :]
        variant = jnp.where(blk == 0, 0, jnp.where(blk == nblk - 1, 2, 1))
        s = _dot_nt(q, kw) * scale + bias_ref[0, variant]
        sm = jnp.where(is_meta, _dot_nt(q, km) * scale, NEG)
        m = jnp.maximum(jnp.max(s, axis=-1, keepdims=True), jnp.max(sm, axis=-1, keepdims=True))
        p = jnp.exp(s - m)
        pm = jnp.exp(sm - m)
        l = jnp.sum(p, axis=-1, keepdims=True) + jnp.sum(pm, axis=-1, keepdims=True)
        o = (_dot(p.astype(BF16), vw) + _dot(pm.astype(BF16), vm)) / l
        o_ref[pl.ds(pl.multiple_of(blk * NA_QN, NA_QN), NA_QN), :] = o.astype(o_ref.dtype)
        return carry

    lax.fori_loop(0, nblk, body, 0)


def _na_meta_kernel(q_ref, km_ref, vm_ref, o_ref):
    is_meta = lax.broadcasted_iota(jnp.int32, (1, META_BLK), 1) >= META_OFF
    s = jnp.where(is_meta, _dot_nt(q_ref[...], km_ref[...]) * NA_HEAD ** -0.5, NEG)
    p = jnp.exp(s - jnp.max(s, axis=-1, keepdims=True))
    o = _dot(p.astype(BF16), vm_ref[...]) / jnp.sum(p, axis=-1, keepdims=True)
    o_ref[...] = o.astype(o_ref.dtype)


def _neighborhood_attention(p_na, rpb, B, S):
    rows = S // GRID_W
    assert rows % NA_QROWS == 0 and rows >= NA_KROWS
    bias = _na_bias_tables(rpb)
    meta_blk = B * S // META_BLK
    y = pl.pallas_call(
        functools.partial(_na_kernel, rows=rows),
        grid=(B, NA_HEADS),
        in_specs=[pl.BlockSpec((S, NA_HEAD), lambda b, h: (b, h)),
                  pl.BlockSpec((S, NA_HEAD), lambda b, h: (b, NA_HEADS + h)),
                  pl.BlockSpec((S, NA_HEAD), lambda b, h: (b, 2 * NA_HEADS + h)),
                  pl.BlockSpec((META_BLK, NA_HEAD), lambda b, h: (meta_blk + b, NA_HEADS + h)),
                  pl.BlockSpec((META_BLK, NA_HEAD), lambda b, h: (meta_blk + b, 2 * NA_HEADS + h)),
                  pl.BlockSpec((1, 3, NA_QN, NA_KN), lambda b, h: (h, 0, 0, 0))],
        out_specs=pl.BlockSpec((S, NA_HEAD), lambda b, h: (b, h)),
        out_shape=jax.ShapeDtypeStruct((B * S, NA_WIDTH), BF16),
        compiler_params=_cparams(("parallel", "arbitrary")),
        name="na_window",
    )(p_na, p_na, p_na, p_na, p_na, bias)
    y_meta = pl.pallas_call(
        _na_meta_kernel,
        grid=(B, NA_HEADS),
        in_specs=[pl.BlockSpec((META_BLK, NA_HEAD), lambda b, h: (meta_blk + b, h)),
                  pl.BlockSpec((META_BLK, NA_HEAD), lambda b, h: (meta_blk + b, NA_HEADS + h)),
                  pl.BlockSpec((META_BLK, NA_HEAD), lambda b, h: (meta_blk + b, 2 * NA_HEADS + h))],
        out_specs=pl.BlockSpec((META_BLK, NA_HEAD), lambda b, h: (b, h)),
        out_shape=jax.ShapeDtypeStruct((B * META_BLK, NA_WIDTH), BF16),
        compiler_params=_cparams(("parallel", "arbitrary")),
        name="na_meta",
    )(p_na, p_na, p_na)
    return jnp.concatenate([y, y_meta], axis=0)


def _pair_sum_matrix(value):
    r = np.arange(LANE)[:, None] // RW_HEAD
    c = np.arange(LANE)[None, :] // RW_HEAD
    assert math.log2(value).is_integer()
    return jnp.asarray(np.where(r == c, value, 0.0), BF16)


def _rw_prep_kernel(x_ref, pb_ref, nb_ref, mup_ref, mun_ref, w0_ref, wup_ref, a0_ref, aup_ref, gup_ref,
                    kk_w_ref, ka_ref, ones_ref,
                    r_ref, v_ref, kk_ref, lw_ref, kd_ref, bd_ref, g_ref, *, n_real_tiles, tiles_per_batch):
    i = pl.program_id(0)
    tl = x_ref.shape[0]
    C = RW_WIDTH
    is_meta = i >= n_real_tiles
    last = (i % tiles_per_batch) == tiles_per_batch - 1
    x = x_ref[...]
    rid = lax.broadcasted_iota(jnp.int32, (tl, 1), 0)
    prev = jnp.where(rid == 0, pb_ref[7:8, :], pltpu.roll(x, 1, axis=0))
    nxt = jnp.where(rid == tl - 1, nb_ref[0:1, :], pltpu.roll(x, tl - 1, axis=0))
    no_prev_upto = jnp.where(is_meta, META_OFF, -1)
    no_next_from = jnp.where(jnp.logical_and(jnp.logical_not(is_meta), last), tl - 1, tl)
    prev = jnp.where(rid <= no_prev_upto, 0.0, prev)
    nxt = jnp.where(rid >= no_next_from, 0.0, nxt)
    xs = x + mup_ref[...] * (prev - x) + mun_ref[...] * (nxt - x)

    live = rid >= jnp.where(is_meta, META_OFF, 0)
    r = xs[:, 0:C]
    k = xs[:, C:2 * C]
    v = xs[:, 2 * C:3 * C]
    o = 3 * C
    w_dn = xs[:, o:o + 2 * DECAY_LORA]
    a_dn = xs[:, o + 2 * DECAY_LORA:o + 2 * DECAY_LORA + 2 * ICL_LORA]
    g_dn = xs[:, o + 2 * DECAY_LORA + 2 * ICL_LORA:]
    w_log = w0_ref[...] + _dot(jnp.tanh(w_dn).astype(BF16), wup_ref[...])
    a = jax.nn.sigmoid(a0_ref[...] + _dot(a_dn.astype(BF16), aup_ref[...]))
    g_ref[...] = _dot(jax.nn.sigmoid(g_dn).astype(BF16), gup_ref[...])
    logw = -math.exp(-0.5) * jax.nn.sigmoid(w_log)
    kk_raw = k * kk_w_ref[...]
    ka = ka_ref[...]
    ones = ones_ref[...]
    for hp in range(RW_PAIRS):
        sl = slice(hp * LANE, (hp + 1) * LANE)
        kr = kk_raw[:, sl]
        nrm = jnp.sqrt(_dot_split_lhs(kr * kr, ones))
        kk = jnp.where(live, kr / jnp.maximum(nrm, 1e-12), 0.0)
        r_ref[hp] = jnp.where(live, r[:, sl], 0.0)
        v_ref[hp] = jnp.where(live, v[:, sl], 0.0)
        kk_ref[hp] = kk
        kp = k[:, sl]
        for d in range(2):
            sld = slice(d * C + hp * LANE, d * C + (hp + 1) * LANE)
            ad = a[:, sld]
            lw_ref[d, hp] = jnp.where(live, logw[:, sld], 0.0)
            kd_ref[d, hp] = jnp.where(live, kp * (1.0 + (ad - 1.0) * ka[:, sl]), 0.0)
            bd_ref[d, hp] = kk * ad


def _rw_prep(p_rw, prm, B, S):
    T = p_rw.shape[0]
    tl = PREP_ROWS
    n_real_tiles = B * S // tl
    tiles_per_batch = S // tl
    sub = tl // 8

    def prev_map(i):
        b = i // tiles_per_batch
        meta_last = (B * S + META_BLK * b + META_BLK - 8) // 8
        real = jnp.where(i % tiles_per_batch == 0, meta_last, i * sub - 1)
        return (jnp.where(i < n_real_tiles, real, jnp.maximum(i * sub - 1, 0)), 0)

    def next_map(i):
        real = jnp.where(i % tiles_per_batch == tiles_per_batch - 1, 0, (i + 1) * sub)
        return (jnp.where(i < n_real_tiles, real, (i - n_real_tiles) * (S // 8)), 0)

    fix = lambda i: (0, 0)
    pair = pl.BlockSpec((RW_PAIRS, tl, LANE), lambda i: (0, i, 0))
    pair2 = pl.BlockSpec((2, RW_PAIRS, tl, LANE), lambda i: (0, 0, i, 0))
    pair_shape = jax.ShapeDtypeStruct((RW_PAIRS, T, LANE), F32)
    pair2_shape = jax.ShapeDtypeStruct((2, RW_PAIRS, T, LANE), F32)
    C = RW_WIDTH
    return pl.pallas_call(
        functools.partial(_rw_prep_kernel, n_real_tiles=n_real_tiles, tiles_per_batch=tiles_per_batch),
        grid=(T // tl,),
        in_specs=[pl.BlockSpec((tl, RW_PROJ_PAD), lambda i: (i, 0)),
                  pl.BlockSpec((8, RW_PROJ_PAD), prev_map),
                  pl.BlockSpec((8, RW_PROJ_PAD), next_map),
                  pl.BlockSpec((1, RW_PROJ_PAD), fix),
                  pl.BlockSpec((1, RW_PROJ_PAD), fix),
                  pl.BlockSpec((1, 2 * C), fix),
                  pl.BlockSpec((2 * DECAY_LORA, 2 * C), fix),
                  pl.BlockSpec((1, 2 * C), fix),
                  pl.BlockSpec((2 * ICL_LORA, 2 * C), fix),
                  pl.BlockSpec((GATE_PAD, C), fix),
                  pl.BlockSpec((1, C), fix),
                  pl.BlockSpec((1, C), fix),
                  pl.BlockSpec((LANE, LANE), fix)],
        out_specs=[pair, pair, pair, pair2, pair2, pair2, pl.BlockSpec((tl, C), lambda i: (i, 0))],
        out_shape=[pair_shape, pair_shape, pair_shape, pair2_shape, pair2_shape, pair2_shape,
                   jax.ShapeDtypeStruct((T, C), F32)],
        compiler_params=_cparams(("parallel",)),
        name="rwkv_prep",
    )(p_rw, p_rw, p_rw, prm["mu_prev"], prm["mu_next"], prm["w0"], prm["w_up"], prm["a0"], prm["a_up"],
      prm["g_up"], prm["k_k"], prm["k_a"], _pair_sum_matrix(1.0))


SCAN_GROUP = 12


def _rw_scan_kernel(r_ref, v_ref, kk_ref, lw_ref, kd_ref, bd_ref, o_ref, s_sc):
    d = pl.program_id(0) % 2
    c = pl.program_id(1)

    @pl.when(c == 0)
    def _():
        s_sc[...] = jnp.zeros(s_sc.shape, F32)

    P2 = 2 * CHUNK
    sign = 1 - 2 * d
    row = lax.broadcasted_iota(jnp.int32, (CHUNK, CHUNK), 0)
    col = lax.broadcasted_iota(jnp.int32, (CHUNK, CHUNK), 1)
    tri = jnp.where((row - col) * sign >= 0, 1.0, 0.0).astype(BF16)
    ri = lax.broadcasted_iota(jnp.int32, (P2, P2), 0)
    ci = lax.broadcasted_iota(jnp.int32, (P2, P2), 1)
    same = (ri // CHUNK) == (ci // CHUNK)
    lag = (ri % CHUNK - ci % CHUNK) * sign
    strict = jnp.logical_and(same, lag > 0)
    eye = jnp.where(ri == ci, 1.0, 0.0)
    ri2 = lax.broadcasted_iota(jnp.int32, (P2, 2 * P2), 0)
    ci2 = lax.broadcasted_iota(jnp.int32, (P2, 2 * P2), 1) % P2
    incl2 = jnp.logical_and((ri2 // CHUNK) == (ci2 // CHUNK), (ri2 % CHUNK - ci2 % CHUNK) * sign >= 0)
    head0 = lax.broadcasted_iota(jnp.int32, (CHUNK, P2), 1) < RW_HEAD
    n_levels = int(math.log2(CHUNK)) - 1

    def stack(z):
        return jnp.concatenate([jnp.where(head0, z, 0.0), jnp.where(head0, 0.0, z)], axis=0)

    def prep(hp):
        lw = lw_ref[0, hp]
        cum = sum(_dot(tri, t) for t in _bf16_terms(lw, 2))
        p_inv = jnp.exp(-cum)
        x = jnp.concatenate([stack(-kk_ref[hp] * jnp.exp(cum - lw)),
                             stack(r_ref[hp] * jnp.exp(cum))], axis=0).astype(BF16)
        y = jnp.concatenate([stack(bd_ref[0, hp] * p_inv),
                             stack(kd_ref[0, hp] * p_inv)], axis=0).astype(BF16)
        vbd = stack(v_ref[hp]).astype(BF16)
        p_tot = jnp.exp(jnp.sum(lw, axis=0, keepdims=True))
        return x, y, vbd, p_tot

    for g0 in range(0, RW_PAIRS, SCAN_GROUP):
        hps = list(range(g0, g0 + SCAN_GROUP))
        pre = [prep(hp) for hp in hps]
        xs_ = [p[0] for p in pre]
        ys_ = [p[1] for p in pre]
        vs_ = [p[2] for p in pre]
        s_old = [s_sc[hp] for hp in hps]
        gram = [_dot_nt(x, y) for x, y in zip(xs_, ys_)]
        xst = [_dot_nt(x, s.astype(BF16)) for x, s in zip(xs_, s_old)]
        a_ab = [jnp.where(strict, g[:P2, :P2], 0.0) for g in gram]
        t_inv = [eye + a for a in a_ab]
        pw = [a.astype(BF16) for a in a_ab]
        for _ in range(n_levels):
            pw = [_dot(p, p).astype(BF16) for p in pw]
            t_inv = [t + _dot(t.astype(BF16), p) for t, p in zip(t_inv, pw)]
        rhs = [xt[:P2] + _dot(jnp.where(strict, g[:P2, P2:], 0.0).astype(BF16), vb)
               for xt, g, vb in zip(xst, gram, vs_)]
        u = [_dot(t.astype(BF16), r_.astype(BF16)) for t, r_ in zip(t_inv, rhs)]
        uv = [jnp.concatenate([u_.astype(BF16), vb], axis=0) for u_, vb in zip(u, vs_)]
        for i, hp in enumerate(hps):
            a_r = jnp.where(incl2, gram[i][P2:, :], 0.0).astype(BF16)
            o_bd = xst[i][P2:] + _dot(a_r, uv[i])
            o_ref[0, hp] = o_bd[:CHUNK] + o_bd[CHUNK:]
            s_sc[hp] = (s_old[i] + _dot_tn(uv[i], ys_[i])) * pre[i][3]


def _rw_scan(r, v, kk, lw, kd, bd, B, S):
    T = r.shape[1]
    nc = S // CHUNK + 1
    meta_chunk = (B * S + META_BLK - CHUNK) // CHUNK

    def blk(g, c):
        b = g // 2
        j = jnp.where(g % 2 == 0, c, nc - 1 - c)
        return jnp.where(j == 0, meta_chunk + b * (META_BLK // CHUNK), b * (S // CHUNK) + j - 1)

    shared = pl.BlockSpec((RW_PAIRS, CHUNK, LANE), lambda g, c: (0, blk(g, c), 0))
    per_dir = pl.BlockSpec((1, RW_PAIRS, CHUNK, LANE), lambda g, c: (g % 2, 0, blk(g, c), 0))
    return pl.pallas_call(
        _rw_scan_kernel,
        grid=(2 * B, nc),
        in_specs=[shared, shared, shared, per_dir, per_dir, per_dir],
        out_specs=per_dir,
        out_shape=jax.ShapeDtypeStruct((2, RW_PAIRS, T, LANE), F32),
        scratch_shapes=[pltpu.VMEM((RW_PAIRS, LANE, LANE), F32)],
        compiler_params=_cparams(("parallel", "arbitrary")),
        name="rwkv_scan",
    )(r, v, kk, lw, kd, bd)


def _rw_post_kernel(o_ref, r_ref, v_ref, kd_ref, g_ref, rk_ref, gw_ref, gb_ref, mean_ref, ones_ref, y_ref, *,
                    n_real_tiles):
    mean_m = mean_ref[...]
    ones = ones_ref[...]
    tl = y_ref.shape[0]
    rid = lax.broadcasted_iota(jnp.int32, (tl, 1), 0) % META_BLK
    live = rid >= jnp.where(pl.program_id(0) >= n_real_tiles, META_OFF, 0)
    for hp in range(RW_PAIRS):
        sl = slice(hp * LANE, (hp + 1) * LANE)
        y = o_ref[0, hp] + o_ref[1, hp]
        yc = y - _dot_split_lhs(y, mean_m)
        var = _dot_split_lhs(yc * yc, mean_m)
        yn = yc * lax.rsqrt(var + GN_EPS) * gw_ref[:, sl] + gb_ref[:, sl]
        k_bonus = 0.5 * (kd_ref[0, hp] + kd_ref[1, hp])
        bonus = _dot_split_lhs(r_ref[hp] * k_bonus * rk_ref[:, sl], ones) * v_ref[hp]
        y_ref[:, sl] = jnp.where(live, (yn + bonus) * g_ref[:, sl], 0.0).astype(y_ref.dtype)


def _rw_post(o, r, v, kd, g, prm, n_real):
    T = g.shape[0]
    tl = _row_tile(T, 256)
    assert n_real % tl == 0
    C = RW_WIDTH
    fix = lambda i: (0, 0)
    pair = pl.BlockSpec((RW_PAIRS, tl, LANE), lambda i: (0, i, 0))
    pair2 = pl.BlockSpec((2, RW_PAIRS, tl, LANE), lambda i: (0, 0, i, 0))
    return pl.pallas_call(
        functools.partial(_rw_post_kernel, n_real_tiles=n_real // tl),
        grid=(T // tl,),
        in_specs=[pair2, pair, pair, pair2,
                  pl.BlockSpec((tl, C), lambda i: (i, 0)),
                  pl.BlockSpec((1, C), fix), pl.BlockSpec((1, C), fix), pl.BlockSpec((1, C), fix),
                  pl.BlockSpec((LANE, LANE), fix), pl.BlockSpec((LANE, LANE), fix)],
        out_specs=pl.BlockSpec((tl, C), lambda i: (i, 0)),
        out_shape=jax.ShapeDtypeStruct((T, C), BF16),
        compiler_params=_cparams(("parallel",)),
        name="rwkv_post",
    )(o, r, v, kd, g, prm["r_k"], prm["gn_w"], prm["gn_b"],
      _pair_sum_matrix(1.0 / RW_HEAD), _pair_sum_matrix(1.0))


def _block_diag2(m):
    z = jnp.zeros_like(m[0])
    return jnp.concatenate([jnp.concatenate([m[0], z], axis=1), jnp.concatenate([z, m[1]], axis=1)], axis=0)


def _rwkv7(p_rw, mu_prev, mu_next, w0, w_up, a0, a_up, g_up, k_k, k_a, r_k, gn_w, gn_b, B, S):
    C = RW_WIDTH
    pad = RW_PROJ_PAD - RW_PROJ
    prm = {
        "mu_prev": jnp.pad(mu_prev, (0, pad)).reshape(1, RW_PROJ_PAD),
        "mu_next": jnp.pad(mu_next, (0, pad)).reshape(1, RW_PROJ_PAD),
        "w0": w0.reshape(1, 2 * C),
        "w_up": _block_diag2(w_up).astype(BF16),
        "a0": a0.reshape(1, 2 * C),
        "a_up": _block_diag2(a_up).astype(BF16),
        "g_up": jnp.pad(g_up, ((0, GATE_PAD - GATE_LORA), (0, 0))).astype(BF16),
        "k_k": k_k.reshape(1, C),
        "k_a": k_a.reshape(1, C),
        "r_k": r_k.reshape(1, C),
        "gn_w": gn_w.reshape(1, C),
        "gn_b": gn_b.reshape(1, C),
    }
    r, v, kk, lw, kd, bd, g = _rw_prep(p_rw, prm, B, S)
    o = _rw_scan(r, v, kk, lw, kd, bd, B, S)
    return _rw_post(o, r, v, kd, g, prm, B * S)


def kernel(x, meta_tokens, ffn1_w_in, ffn1_w_out, ln_ffn1_g, ln_ffn1_b, w_in, rw_mu_prev, rw_mu_next, rw_w0, rw_w_up, rw_a0, rw_a_up, rw_g_up, rw_k_k, rw_k_a, rw_r_k, rw_gn_w, rw_gn_b, na_rpb, gq_q_gain, gq_k_gain, w_out, ln_mix_g, ln_mix_b, ffn2_w_in, ffn2_w_out, ln_ffn2_g, ln_ffn2_b):
    B, S, D = x.shape
    assert D == D_MODEL and S % 1024 == 0
    meta = jnp.zeros((B, META_BLK, D), x.dtype).at[:, META_OFF:].set(
        jnp.broadcast_to(meta_tokens[None].astype(x.dtype), (B, N_META, D)))
    h = jnp.concatenate([x.reshape(B * S, D), meta.reshape(B * META_BLK, D)], axis=0)
    hb = h.astype(BF16)
    cos_t, sin_t = _rope_tables(B, S)
    for l in range(DEPTH):
        h, hb = _ffn(h, hb, ffn1_w_in[l], ffn1_w_out[l], ln_ffn1_g[l], ln_ffn1_b[l])
        w = w_in[l]
        w_rw = jnp.pad(w[:, :RW_PROJ], ((0, 0), (0, RW_PROJ_PAD - RW_PROJ))).astype(BF16)
        w_na = w[:, RW_PROJ:RW_PROJ + NA_PROJ].astype(BF16)
        w_gq = w[:, RW_PROJ + NA_PROJ:].astype(BF16)
        p_rw = _matmul(hb, w_rw, F32)
        p_na = _matmul(hb, w_na, BF16)
        p_gq = _matmul(hb, w_gq, F32)
        y_a = _rwkv7(p_rw, rw_mu_prev[l], rw_mu_next[l], rw_w0[l], rw_w_up[l], rw_a0[l], rw_a_up[l],
                     rw_g_up[l], rw_k_k[l], rw_k_a[l], rw_r_k[l], rw_gn_w[l], rw_gn_b[l], B, S)
        y_b = _neighborhood_attention(p_na, na_rpb[l], B, S)
        q, k, v = _gq_prep(p_gq, cos_t, sin_t, gq_q_gain[l], gq_k_gain[l])
        y_c = _gqa(q, k, v, B, S)
        z = _mix_out_matmul(y_a, y_b, y_c, w_out[l].astype(BF16), h)
        h, hb = _layer_norm(z, ln_mix_g[l], ln_mix_b[l])
        h, hb = _ffn(h, hb, ffn2_w_in[l], ffn2_w_out[l], ln_ffn2_g[l], ln_ffn2_b[l])
    return h[:B * S].reshape(B, S, D)
```

```python
import functools
import math

import numpy as np
import jax
import jax.numpy as jnp
from jax import lax
from jax.experimental import pallas as pl
from jax.experimental.pallas import tpu as pltpu

F32 = jnp.float32
BF16 = jnp.bfloat16
HIGHEST = lax.Precision.HIGHEST

D_MODEL = 4096
DEPTH = 4
N_META = 16
GRID_W = 64
RW_HEAD = 64
RW_WIDTH = 1536
RW_HEADS = RW_WIDTH // RW_HEAD
RW_PAIRS = RW_HEADS // 2
DECAY_LORA = 64
ICL_LORA = 64
GATE_LORA = 224
RW_PROJ = 3 * RW_WIDTH + 2 * DECAY_LORA + 2 * ICL_LORA + GATE_LORA
GN_EPS = 64e-5
NA_HEAD = 128
NA_WIDTH = 1536
NA_HEADS = NA_WIDTH // NA_HEAD
NA_KR = 8
NA_KC = 16
NA_PROJ = 3 * NA_WIDTH
GQ_HEAD = 128
GQ_WIDTH = 1024
GQ_HEADS = GQ_WIDTH // GQ_HEAD
GQ_KV_HEADS = 2
GQ_GROUP = GQ_HEADS // GQ_KV_HEADS
GQ_KV_W = GQ_KV_HEADS * GQ_HEAD
GQ_PROJ = GQ_WIDTH + 2 * GQ_KV_W
ROPE_THETA = 10000.0
LOG2E = math.log2(math.e)
QK_EPS = 1e-6
D_FF = 3 * D_MODEL // 2
LN_EPS = 1e-5
ALPHA = (2.0 * DEPTH) ** 0.25

LANE = 128
META_BLK = 128
META_OFF = META_BLK - N_META
RW_PROJ_PAD = 5120
GATE_PAD = 256
CHUNK = 64
PREP_ROWS = 128
GQ_KV_TILE = 8192
NEG = -1e30
VMEM_LIMIT = 56 * 2 ** 20


def _cparams(sem):
    return pltpu.CompilerParams(dimension_semantics=sem, vmem_limit_bytes=VMEM_LIMIT)


def _row_tile(total, cap):
    best = LANE
    for t in range(LANE, cap + 1, LANE):
        if total % t == 0:
            best = t
    return best


def _dot_nt(a, b, precision=None):
    return lax.dot_general(a, b, (((1,), (1,)), ((), ())), preferred_element_type=F32, precision=precision)


def _dot_tn(a, b, precision=None):
    return lax.dot_general(a, b, (((0,), (0,)), ((), ())), preferred_element_type=F32, precision=precision)


def _dot(a, b, precision=None):
    return jnp.dot(a, b, preferred_element_type=F32, precision=precision)


def _bf16_terms(a, n):
    terms = []
    for _ in range(n - 1):
        t = a.astype(BF16)
        terms.append(t)
        a = a - t.astype(F32)
    terms.append(a.astype(BF16))
    return terms


def _dot_split_lhs(a, m_bf16, n=3):
    return sum(_dot(t, m_bf16) for t in _bf16_terms(a, n))


def _mm_kernel(x_ref, w_ref, o_ref):
    o_ref[...] = _dot(x_ref[...], w_ref[...]).astype(o_ref.dtype)


def _matmul(x, w, out_dtype, tn=512):
    T, K = x.shape
    N = w.shape[1]
    tm = _row_tile(T, 1280)
    return pl.pallas_call(
        _mm_kernel,
        grid=(T // tm, N // tn),
        in_specs=[pl.BlockSpec((tm, K), lambda i, j: (i, 0)),
                  pl.BlockSpec((K, tn), lambda i, j: (0, j))],
        out_specs=pl.BlockSpec((tm, tn), lambda i, j: (i, j)),
        out_shape=jax.ShapeDtypeStruct((T, N), out_dtype),
        compiler_params=_cparams(("parallel", "arbitrary")),
        name="proj_matmul",
    )(x, w)


def _swiglu_kernel(x_ref, wg_ref, wu_ref, o_ref):
    x = x_ref[...]
    g = _dot(x, wg_ref[...])
    u = _dot(x, wu_ref[...])
    o_ref[...] = (g * jax.nn.sigmoid(g) * u).astype(o_ref.dtype)


def _swiglu_matmul(x, w_in, tn=512):
    T, K = x.shape
    F = w_in.shape[1] // 2
    tm = _row_tile(T, 1280)
    nj = F // tn
    return pl.pallas_call(
        _swiglu_kernel,
        grid=(T // tm, nj),
        in_specs=[pl.BlockSpec((tm, K), lambda i, j: (i, 0)),
                  pl.BlockSpec((K, tn), lambda i, j: (0, j)),
                  pl.BlockSpec((K, tn), lambda i, j: (0, j + nj))],
        out_specs=pl.BlockSpec((tm, tn), lambda i, j: (i, j)),
        out_shape=jax.ShapeDtypeStruct((T, F), BF16),
        compiler_params=_cparams(("parallel", "arbitrary")),
        name="ffn_swiglu",
    )(x, w_in, w_in)


def _mm_resid_kernel(x_ref, w_ref, h_ref, o_ref, *, scale):
    o_ref[...] = ALPHA * h_ref[...] + scale * _dot(x_ref[...], w_ref[...])


def _matmul_resid(x, w, h, scale, tn=512):
    T, K = x.shape
    N = w.shape[1]
    tm = _row_tile(T, 640)
    return pl.pallas_call(
        functools.partial(_mm_resid_kernel, scale=scale),
        grid=(T // tm, N // tn),
        in_specs=[pl.BlockSpec((tm, K), lambda i, j: (i, 0)),
                  pl.BlockSpec((K, tn), lambda i, j: (0, j)),
                  pl.BlockSpec((tm, tn), lambda i, j: (i, j))],
        out_specs=pl.BlockSpec((tm, tn), lambda i, j: (i, j)),
        out_shape=jax.ShapeDtypeStruct((T, N), F32),
        compiler_params=_cparams(("parallel", "arbitrary")),
        name="resid_matmul",
    )(x, w, h)


def _mix_out_kernel(ya_ref, yb_ref, yc_ref, wa_ref, wb_ref, wc_ref, h_ref, o_ref):
    mix = _dot(ya_ref[...], wa_ref[...]) + _dot(yb_ref[...], wb_ref[...]) + _dot(yc_ref[...], wc_ref[...])
    o_ref[...] = ALPHA * h_ref[...] + mix


def _mix_out_matmul(y_a, y_b, y_c, w, h, tn=512):
    T = h.shape[0]
    N = w.shape[1]
    assert RW_WIDTH == NA_WIDTH and (RW_WIDTH + NA_WIDTH) % GQ_WIDTH == 0
    tm = _row_tile(T, 640)
    row = lambda i, j: (i, 0)
    return pl.pallas_call(
        _mix_out_kernel,
        grid=(T // tm, N // tn),
        in_specs=[pl.BlockSpec((tm, RW_WIDTH), row),
                  pl.BlockSpec((tm, NA_WIDTH), row),
                  pl.BlockSpec((tm, GQ_WIDTH), row),
                  pl.BlockSpec((RW_WIDTH, tn), lambda i, j: (0, j)),
                  pl.BlockSpec((NA_WIDTH, tn), lambda i, j: (1, j)),
                  pl.BlockSpec((GQ_WIDTH, tn), lambda i, j: ((RW_WIDTH + NA_WIDTH) // GQ_WIDTH, j)),
                  pl.BlockSpec((tm, tn), lambda i, j: (i, j))],
        out_specs=pl.BlockSpec((tm, tn), lambda i, j: (i, j)),
        out_shape=jax.ShapeDtypeStruct((T, N), F32),
        compiler_params=_cparams(("parallel", "arbitrary")),
        name="mix_out_matmul",
    )(y_a, y_b, y_c, w, w, w, h)


def _ln_kernel(z_ref, g_ref, b_ref, h_ref, hb_ref):
    z = z_ref[...]
    mu = jnp.mean(z, axis=-1, keepdims=True)
    zc = z - mu
    var = jnp.mean(zc * zc, axis=-1, keepdims=True)
    y = zc * lax.rsqrt(var + LN_EPS) * g_ref[...] + b_ref[...]
    h_ref[...] = y
    hb_ref[...] = y.astype(BF16)


def _layer_norm(z, g, b):
    T, D = z.shape
    tm = _row_tile(T, 256)
    return pl.pallas_call(
        _ln_kernel,
        grid=(T // tm,),
        in_specs=[pl.BlockSpec((tm, D), lambda i: (i, 0)),
                  pl.BlockSpec((1, D), lambda i: (0, 0)),
                  pl.BlockSpec((1, D), lambda i: (0, 0))],
        out_specs=[pl.BlockSpec((tm, D), lambda i: (i, 0)),
                   pl.BlockSpec((tm, D), lambda i: (i, 0))],
        out_shape=[jax.ShapeDtypeStruct((T, D), F32), jax.ShapeDtypeStruct((T, D), BF16)],
        compiler_params=_cparams(("parallel",)),
        name="layer_norm",
    )(z, g.reshape(1, D), b.reshape(1, D))


def _ffn(h, hb, w_in, w_out, g, b):
    act = _swiglu_matmul(hb, w_in.astype(BF16))
    z = _matmul_resid(act, w_out.astype(BF16), h, 0.5)
    return _layer_norm(z, g, b)


def _rope_tables(B, S):
    t = jnp.arange(S)
    row = (t // GRID_W).astype(F32)
    col = (t % GRID_W).astype(F32)
    half = GQ_HEAD // 4
    inv = ROPE_THETA ** (-jnp.arange(half, dtype=F32) / half)
    ang = jnp.concatenate([row[:, None] * inv, col[:, None] * inv], axis=-1)
    cos, sin = jnp.cos(ang), jnp.sin(ang)
    cos2 = jnp.concatenate([cos, cos], axis=-1)
    sin2 = jnp.concatenate([-sin, sin], axis=-1)
    pad_c = jnp.ones((B * META_BLK, GQ_HEAD), F32)
    pad_s = jnp.zeros((B * META_BLK, GQ_HEAD), F32)
    return (jnp.concatenate([jnp.tile(cos2, (B, 1)), pad_c], axis=0),
            jnp.concatenate([jnp.tile(sin2, (B, 1)), pad_s], axis=0))


def _gq_prep_kernel(p_ref, cos_ref, sin_ref, qg_ref, kg_ref, q_ref, k_ref, v_ref):
    cos = cos_ref[...]
    sin = sin_ref[...]

    def rms_rope(x, gain):
        x = x * lax.rsqrt(jnp.mean(x * x, axis=-1, keepdims=True) + QK_EPS) * gain
        return x * cos + pltpu.roll(x, GQ_HEAD // 2, axis=1) * sin

    for hd in range(GQ_HEADS):
        sl = slice(hd * GQ_HEAD, (hd + 1) * GQ_HEAD)
        q_ref[:, sl] = (rms_rope(p_ref[:, sl], qg_ref[...]) * (GQ_HEAD ** -0.5 * LOG2E)).astype(BF16)
    for hd in range(GQ_KV_HEADS):
        sl = slice(hd * GQ_HEAD, (hd + 1) * GQ_HEAD)
        slk = slice(GQ_WIDTH + hd * GQ_HEAD, GQ_WIDTH + (hd + 1) * GQ_HEAD)
        slv = slice(GQ_WIDTH + GQ_KV_W + hd * GQ_HEAD, GQ_WIDTH + GQ_KV_W + (hd + 1) * GQ_HEAD)
        k_ref[:, sl] = rms_rope(p_ref[:, slk], kg_ref[...]).astype(BF16)
        v_ref[:, sl] = p_ref[:, slv].astype(BF16)


def _gq_prep(p_gq, cos_t, sin_t, q_gain, k_gain):
    T = p_gq.shape[0]
    tm = _row_tile(T, 256)
    row = lambda i: (i, 0)
    fix = lambda i: (0, 0)
    return pl.pallas_call(
        _gq_prep_kernel,
        grid=(T // tm,),
        in_specs=[pl.BlockSpec((tm, GQ_PROJ), row),
                  pl.BlockSpec((tm, GQ_HEAD), row),
                  pl.BlockSpec((tm, GQ_HEAD), row),
                  pl.BlockSpec((1, GQ_HEAD), fix),
                  pl.BlockSpec((1, GQ_HEAD), fix)],
        out_specs=[pl.BlockSpec((tm, GQ_WIDTH), row),
                   pl.BlockSpec((tm, GQ_KV_W), row),
                   pl.BlockSpec((tm, GQ_KV_W), row)],
        out_shape=[jax.ShapeDtypeStruct((T, GQ_WIDTH), BF16),
                   jax.ShapeDtypeStruct((T, GQ_KV_W), BF16),
                   jax.ShapeDtypeStruct((T, GQ_KV_W), BF16)],
        compiler_params=_cparams(("parallel",)),
        name="gqa_prep",
    )(p_gq, cos_t, sin_t, q_gain.reshape(1, GQ_HEAD), k_gain.reshape(1, GQ_HEAD))


def _flash_kernel(q_ref, k_ref, v_ref, km_ref, vm_ref, o_ref, m_sc, acc_sc, *, nk):
    ki = pl.program_id(3)
    qs = [q_ref[:, g * GQ_HEAD:(g + 1) * GQ_HEAD] for g in range(GQ_GROUP)]

    def attend(k, v, mask):
        v_ones = jnp.concatenate([v, jnp.ones(v.shape, BF16)], axis=1)

        def scores(g):
            s = _dot_nt(qs[g], k)
            return s if mask is None else jnp.where(mask, s, NEG)

        def update(g, s):
            m_prev = m_sc[g]
            m_new = jnp.maximum(m_prev, jnp.max(s, axis=-1, keepdims=True))
            p = jnp.exp2(s - m_new).astype(BF16)
            acc_sc[g] = jnp.exp2(m_prev - m_new) * acc_sc[g] + _dot(p, v_ones)
            m_sc[g] = m_new

        s_next = scores(0)
        for g in range(GQ_GROUP):
            s_cur = s_next
            if g + 1 < GQ_GROUP:
                s_next = scores(g + 1)
            update(g, s_cur)

    @pl.when(ki == 0)
    def _():
        m_sc[...] = jnp.full(m_sc.shape, NEG, F32)
        acc_sc[...] = jnp.zeros(acc_sc.shape, F32)
        is_meta = lax.broadcasted_iota(jnp.int32, (1, META_BLK), 1) >= META_OFF
        attend(km_ref[...], vm_ref[...], is_meta)

    attend(k_ref[...], v_ref[...], None)

    @pl.when(ki == nk - 1)
    def _():
        for g in range(GQ_GROUP):
            o = acc_sc[g, :, :GQ_HEAD] / acc_sc[g, :, GQ_HEAD:]
            o_ref[:, g * GQ_HEAD:(g + 1) * GQ_HEAD] = o.astype(o_ref.dtype)


def _gqa(q, k, v, B, S):
    T = q.shape[0]
    tq = 128
    tk = _row_tile(S, GQ_KV_TILE)
    nq_real = S // tq
    nq = nq_real + META_BLK // tq
    nk = S // tk
    meta_q = B * S // tq
    meta_k = B * S // META_BLK

    def q_map(b, g, qi, ki):
        return (jnp.where(qi < nq_real, b * nq_real + qi, meta_q + b * (META_BLK // tq) + (qi - nq_real)), g)

    return pl.pallas_call(
        functools.partial(_flash_kernel, nk=nk),
        grid=(B, GQ_KV_HEADS, nq, nk),
        in_specs=[pl.BlockSpec((tq, GQ_GROUP * GQ_HEAD), q_map),
                  pl.BlockSpec((tk, GQ_HEAD), lambda b, g, qi, ki: (b * nk + ki, g)),
                  pl.BlockSpec((tk, GQ_HEAD), lambda b, g, qi, ki: (b * nk + ki, g)),
                  pl.BlockSpec((META_BLK, GQ_HEAD), lambda b, g, qi, ki: (meta_k + b, g)),
                  pl.BlockSpec((META_BLK, GQ_HEAD), lambda b, g, qi, ki: (meta_k + b, g))],
        out_specs=pl.BlockSpec((tq, GQ_GROUP * GQ_HEAD), q_map),
        out_shape=jax.ShapeDtypeStruct((T, GQ_WIDTH), BF16),
        scratch_shapes=[pltpu.VMEM((GQ_GROUP, tq, 1), F32),
                        pltpu.VMEM((GQ_GROUP, tq, 2 * GQ_HEAD), F32)],
        compiler_params=_cparams(("parallel", "parallel", "parallel", "arbitrary")),
        name="gqa_flash",
    )(q, k, v, k, v)


NA_QROWS = 8
NA_KROWS = 16
NA_QN = NA_QROWS * GRID_W
NA_KN = NA_KROWS * GRID_W


def _na_bias_tables(rpb):
    qi = np.arange(NA_QROWS)[:, None]
    kr = np.arange(NA_KROWS)[None, :]
    qj = np.arange(GRID_W)[:, None]
    kc = np.arange(GRID_W)[None, :]
    c0 = np.clip(qj - NA_KC // 2, 0, GRID_W - NA_KC)
    valid_c = (kc >= c0) & (kc < c0 + NA_KC)
    col_off = kc - qj + NA_KC - 1
    n_dr, n_dc = 2 * NA_KR - 1, 2 * NA_KC - 1
    col_sel = (col_off[None] == np.arange(n_dc)[:, None, None]) & valid_c[None]
    row_sel, row_ok = [], []
    for i_rel, r0_rel in ((qi, np.maximum(qi - NA_KR // 2, 0)),
                          (qi + NA_KR // 2, qi),
                          (qi + NA_KR, np.minimum(qi + NA_KR // 2, NA_KR))):
        valid_r = (kr >= r0_rel) & (kr < r0_rel + NA_KR)
        row_off = kr - i_rel + NA_KR - 1
        row_sel.append((row_off[..., None] == np.arange(n_dr)) & valid_r[..., None])
        row_ok.append(valid_r)
    row_sel = np.stack(row_sel).astype(np.float32)
    valid = np.stack(row_ok)[:, :, None, :, None] & valid_c[None, None, :, None, :]
    band = jnp.einsum("hrd,dqk->hrqk", rpb, col_sel.astype(np.float32), precision=HIGHEST)
    tab = jnp.einsum("vior,hrqk->hviqok", row_sel, band, precision=HIGHEST)
    window_mask = np.where(valid, 0.0, NEG).astype(np.float32).reshape(3, NA_QN, NA_KN)
    return tab.reshape(NA_HEADS, 3, NA_QN, NA_KN), jnp.asarray(window_mask)


def _na_kernel(q_ref, k_ref, v_ref, km_ref, vm_ref, bias_ref, mask_ref, o_ref, *, rows):
    nblk = rows // NA_QROWS
    scale = NA_HEAD ** -0.5
    km = km_ref[...]
    vm = vm_ref[...]
    is_meta = lax.broadcasted_iota(jnp.int32, (1, META_BLK), 1) >= META_OFF

    def body(blk, carry):
        q = q_ref[pl.ds(pl.multiple_of(blk * NA_QN, NA_QN), NA_QN), :]
        krow = jnp.clip(blk * NA_QROWS - NA_KR // 2, 0, rows - NA_KROWS)
        ks = pl.multiple_of(krow * GRID_W, 4 * GRID_W)
        kw = k_ref[pl.ds(ks, NA_KN), :]
        vw = v_ref[pl.ds(ks, NA_KN), :]
        variant = jnp.where(blk == 0, 0, jnp.where(blk == nblk - 1, 2, 1))
        s = _dot_nt(q, kw) * scale + (bias_ref[0, variant] + mask_ref[variant])
        sm = jnp.where(is_meta, _dot_nt(q, km) * scale, NEG)
        m = jnp.maximum(jnp.max(s, axis=-1, keepdims=True), jnp.max(sm, axis=-1, keepdims=True))
        p = jnp.exp(s - m)
        pm = jnp.exp(sm - m)
        l = jnp.sum(p, axis=-1, keepdims=True) + jnp.sum(pm, axis=-1, keepdims=True)
        o = (_dot(p.astype(BF16), vw) + _dot(pm.astype(BF16), vm)) / l
        o_ref[pl.ds(pl.multiple_of(blk * NA_QN, NA_QN), NA_QN), :] = o.astype(o_ref.dtype)
        return carry

    lax.fori_loop(0, nblk, body, 0)


def _na_meta_kernel(q_ref, km_ref, vm_ref, o_ref):
    is_meta = lax.broadcasted_iota(jnp.int32, (1, META_BLK), 1) >= META_OFF
    s = jnp.where(is_meta, _dot_nt(q_ref[...], km_ref[...]) * NA_HEAD ** -0.5, NEG)
    p = jnp.exp(s - jnp.max(s, axis=-1, keepdims=True))
    o = _dot(p.astype(BF16), vm_ref[...]) / jnp.sum(p, axis=-1, keepdims=True)
    o_ref[...] = o.astype(o_ref.dtype)


def _neighborhood_attention(p_na, rpb, B, S):
    rows = S // GRID_W
    assert rows % NA_QROWS == 0 and rows >= NA_KROWS
    bias, window_mask = _na_bias_tables(rpb)
    meta_blk = B * S // META_BLK
    y = pl.pallas_call(
        functools.partial(_na_kernel, rows=rows),
        grid=(B, NA_HEADS),
        in_specs=[pl.BlockSpec((S, NA_HEAD), lambda b, h: (b, h)),
                  pl.BlockSpec((S, NA_HEAD), lambda b, h: (b, NA_HEADS + h)),
                  pl.BlockSpec((S, NA_HEAD), lambda b, h: (b, 2 * NA_HEADS + h)),
                  pl.BlockSpec((META_BLK, NA_HEAD), lambda b, h: (meta_blk + b, NA_HEADS + h)),
                  pl.BlockSpec((META_BLK, NA_HEAD), lambda b, h: (meta_blk + b, 2 * NA_HEADS + h)),
                  pl.BlockSpec((1, 3, NA_QN, NA_KN), lambda b, h: (h, 0, 0, 0)),
                  pl.BlockSpec((3, NA_QN, NA_KN), lambda b, h: (0, 0, 0))],
        out_specs=pl.BlockSpec((S, NA_HEAD), lambda b, h: (b, h)),
        out_shape=jax.ShapeDtypeStruct((B * S, NA_WIDTH), BF16),
        compiler_params=_cparams(("parallel", "arbitrary")),
        name="na_window",
    )(p_na, p_na, p_na, p_na, p_na, bias, window_mask)
    y_meta = pl.pallas_call(
        _na_meta_kernel,
        grid=(B, NA_HEADS),
        in_specs=[pl.BlockSpec((META_BLK, NA_HEAD), lambda b, h: (meta_blk + b, h)),
                  pl.BlockSpec((META_BLK, NA_HEAD), lambda b, h: (meta_blk + b, NA_HEADS + h)),
                  pl.BlockSpec((META_BLK, NA_HEAD), lambda b, h: (meta_blk + b, 2 * NA_HEADS + h))],
        out_specs=pl.BlockSpec((META_BLK, NA_HEAD), lambda b, h: (b, h)),
        out_shape=jax.ShapeDtypeStruct((B * META_BLK, NA_WIDTH), BF16),
        compiler_params=_cparams(("parallel", "arbitrary")),
        name="na_meta",
    )(p_na, p_na, p_na)
    return jnp.concatenate([y, y_meta], axis=0)


def _pair_sum_matrix(value):
    r = np.arange(LANE)[:, None] // RW_HEAD
    c = np.arange(LANE)[None, :] // RW_HEAD
    assert math.log2(value).is_integer()
    return jnp.asarray(np.where(r == c, value, 0.0), BF16)


def _rw_prep_kernel(x_ref, pb_ref, nb_ref, mup_ref, mun_ref, w0_ref, wup_ref, a0_ref, aup_ref, gup_ref,
                    kk_w_ref, ka_ref, ones_ref,
                    r_ref, v_ref, kk_ref, lw_ref, kd_ref, bd_ref, g_ref, *, n_real_tiles, tiles_per_batch):
    i = pl.program_id(0)
    tl = x_ref.shape[0]
    C = RW_WIDTH
    is_meta = i >= n_real_tiles
    last = (i % tiles_per_batch) == tiles_per_batch - 1
    x = x_ref[...]
    rid = lax.broadcasted_iota(jnp.int32, (tl, 1), 0)
    prev = jnp.where(rid == 0, pb_ref[7:8, :], pltpu.roll(x, 1, axis=0))
    nxt = jnp.where(rid == tl - 1, nb_ref[0:1, :], pltpu.roll(x, tl - 1, axis=0))
    no_prev_upto = jnp.where(is_meta, META_OFF, -1)
    no_next_from = jnp.where(jnp.logical_and(jnp.logical_not(is_meta), last), tl - 1, tl)
    prev = jnp.where(rid <= no_prev_upto, 0.0, prev)
    nxt = jnp.where(rid >= no_next_from, 0.0, nxt)
    xs = x + mup_ref[...] * (prev - x) + mun_ref[...] * (nxt - x)

    live = rid >= jnp.where(is_meta, META_OFF, 0)
    r = xs[:, 0:C]
    k = xs[:, C:2 * C]
    v = xs[:, 2 * C:3 * C]
    o = 3 * C
    w_dn = xs[:, o:o + 2 * DECAY_LORA]
    a_dn = xs[:, o + 2 * DECAY_LORA:o + 2 * DECAY_LORA + 2 * ICL_LORA]
    g_dn = xs[:, o + 2 * DECAY_LORA + 2 * ICL_LORA:]
    w_log = w0_ref[...] + _dot(jnp.tanh(w_dn).astype(BF16), wup_ref[...])
    a = jax.nn.sigmoid(a0_ref[...] + _dot(a_dn.astype(BF16), aup_ref[...]))
    g_ref[...] = _dot(jax.nn.sigmoid(g_dn).astype(BF16), gup_ref[...])
    logw = -math.exp(-0.5) * jax.nn.sigmoid(w_log)
    kk_raw = k * kk_w_ref[...]
    ka = ka_ref[...]
    ones = ones_ref[...]
    for hp in range(RW_PAIRS):
        sl = slice(hp * LANE, (hp + 1) * LANE)
        kr = kk_raw[:, sl]
        nrm = jnp.sqrt(_dot_split_lhs(kr * kr, ones))
        kk = jnp.where(live, kr / jnp.maximum(nrm, 1e-12), 0.0)
        r_ref[hp] = jnp.where(live, r[:, sl], 0.0)
        v_ref[hp] = jnp.where(live, v[:, sl], 0.0)
        kk_ref[hp] = kk
        kp = k[:, sl]
        for d in range(2):
            sld = slice(d * C + hp * LANE, d * C + (hp + 1) * LANE)
            ad = a[:, sld]
            lw_ref[d, hp] = jnp.where(live, logw[:, sld], 0.0)
            kd_ref[d, hp] = jnp.where(live, kp * (1.0 + (ad - 1.0) * ka[:, sl]), 0.0)
            bd_ref[d, hp] = kk * ad


def _rw_prep(p_rw, prm, B, S):
    T = p_rw.shape[0]
    tl = PREP_ROWS
    n_real_tiles = B * S // tl
    tiles_per_batch = S // tl
    sub = tl // 8

    def prev_map(i):
        b = i // tiles_per_batch
        meta_last = (B * S + META_BLK * b + META_BLK - 8) // 8
        real = jnp.where(i % tiles_per_batch == 0, meta_last, i * sub - 1)
        return (jnp.where(i < n_real_tiles, real, jnp.maximum(i * sub - 1, 0)), 0)

    def next_map(i):
        real = jnp.where(i % tiles_per_batch == tiles_per_batch - 1, 0, (i + 1) * sub)
        return (jnp.where(i < n_real_tiles, real, (i - n_real_tiles) * (S // 8)), 0)

    fix = lambda i: (0, 0)
    pair = pl.BlockSpec((RW_PAIRS, tl, LANE), lambda i: (0, i, 0))
    pair2 = pl.BlockSpec((2, RW_PAIRS, tl, LANE), lambda i: (0, 0, i, 0))
    pair_shape = jax.ShapeDtypeStruct((RW_PAIRS, T, LANE), F32)
    pair2_shape = jax.ShapeDtypeStruct((2, RW_PAIRS, T, LANE), F32)
    C = RW_WIDTH
    return pl.pallas_call(
        functools.partial(_rw_prep_kernel, n_real_tiles=n_real_tiles, tiles_per_batch=tiles_per_batch),
        grid=(T // tl,),
        in_specs=[pl.BlockSpec((tl, RW_PROJ_PAD), lambda i: (i, 0)),
                  pl.BlockSpec((8, RW_PROJ_PAD), prev_map),
                  pl.BlockSpec((8, RW_PROJ_PAD), next_map),
                  pl.BlockSpec((1, RW_PROJ_PAD), fix),
                  pl.BlockSpec((1, RW_PROJ_PAD), fix),
                  pl.BlockSpec((1, 2 * C), fix),
                  pl.BlockSpec((2 * DECAY_LORA, 2 * C), fix),
                  pl.BlockSpec((1, 2 * C), fix),
                  pl.BlockSpec((2 * ICL_LORA, 2 * C), fix),
                  pl.BlockSpec((GATE_PAD, C), fix),
                  pl.BlockSpec((1, C), fix),
                  pl.BlockSpec((1, C), fix),
                  pl.BlockSpec((LANE, LANE), fix)],
        out_specs=[pair, pair, pair, pair2, pair2, pair2, pl.BlockSpec((tl, C), lambda i: (i, 0))],
        out_shape=[pair_shape, pair_shape, pair_shape, pair2_shape, pair2_shape, pair2_shape,
                   jax.ShapeDtypeStruct((T, C), F32)],
        compiler_params=_cparams(("parallel",)),
        name="rwkv_prep",
    )(p_rw, p_rw, p_rw, prm["mu_prev"], prm["mu_next"], prm["w0"], prm["w_up"], prm["a0"], prm["a_up"],
      prm["g_up"], prm["k_k"], prm["k_a"], _pair_sum_matrix(1.0))


SCAN_GROUP = 12


def _rw_scan_kernel(r_ref, v_ref, kk_ref, lw_ref, kd_ref, bd_ref, o_ref, s_sc):
    d = pl.program_id(0) % 2
    c = pl.program_id(1)

    @pl.when(c == 0)
    def _():
        s_sc[...] = jnp.zeros(s_sc.shape, F32)

    P2 = 2 * CHUNK
    sign = 1 - 2 * d
    row = lax.broadcasted_iota(jnp.int32, (CHUNK, CHUNK), 0)
    col = lax.broadcasted_iota(jnp.int32, (CHUNK, CHUNK), 1)
    tri = jnp.where((row - col) * sign >= 0, 1.0, 0.0).astype(BF16)
    ri = lax.broadcasted_iota(jnp.int32, (P2, P2), 0)
    ci = lax.broadcasted_iota(jnp.int32, (P2, P2), 1)
    same = (ri // CHUNK) == (ci // CHUNK)
    lag = (ri % CHUNK - ci % CHUNK) * sign
    strict = jnp.logical_and(same, lag > 0)
    eye = jnp.where(ri == ci, 1.0, 0.0)
    ri2 = lax.broadcasted_iota(jnp.int32, (P2, 2 * P2), 0)
    ci2 = lax.broadcasted_iota(jnp.int32, (P2, 2 * P2), 1) % P2
    incl2 = jnp.logical_and((ri2 // CHUNK) == (ci2 // CHUNK), (ri2 % CHUNK - ci2 % CHUNK) * sign >= 0)
    head0 = lax.broadcasted_iota(jnp.int32, (CHUNK, P2), 1) < RW_HEAD
    n_levels = int(math.log2(CHUNK)) - 1

    def stack(z):
        return jnp.concatenate([jnp.where(head0, z, 0.0), jnp.where(head0, 0.0, z)], axis=0)

    def prep(hp):
        lw = lw_ref[0, hp]
        cum = sum(_dot(tri, t) for t in _bf16_terms(lw, 2))
        p_inv = jnp.exp(-cum)
        x = jnp.concatenate([stack(-kk_ref[hp] * jnp.exp(cum - lw)),
                             stack(r_ref[hp] * jnp.exp(cum))], axis=0).astype(BF16)
        y = jnp.concatenate([stack(bd_ref[0, hp] * p_inv),
                             stack(kd_ref[0, hp] * p_inv)], axis=0).astype(BF16)
        vbd = stack(v_ref[hp]).astype(BF16)
        p_tot = jnp.exp(jnp.sum(lw, axis=0, keepdims=True))
        return x, y, vbd, p_tot

    for g0 in range(0, RW_PAIRS, SCAN_GROUP):
        hps = list(range(g0, g0 + SCAN_GROUP))
        pre = [prep(hp) for hp in hps]
        xs_ = [p[0] for p in pre]
        ys_ = [p[1] for p in pre]
        vs_ = [p[2] for p in pre]
        s_old = [s_sc[hp] for hp in hps]
        gram = [_dot_nt(x, y) for x, y in zip(xs_, ys_)]
        xst = [_dot_nt(x, s.astype(BF16)) for x, s in zip(xs_, s_old)]
        a_ab = [jnp.where(strict, g[:P2, :P2], 0.0) for g in gram]
        t_inv = [eye + a for a in a_ab]
        pw = [a.astype(BF16) for a in a_ab]
        for _ in range(n_levels):
            pw = [_dot(p, p).astype(BF16) for p in pw]
            t_inv = [t + _dot(t.astype(BF16), p) for t, p in zip(t_inv, pw)]
        rhs = [xt[:P2] + _dot(jnp.where(strict, g[:P2, P2:], 0.0).astype(BF16), vb)
               for xt, g, vb in zip(xst, gram, vs_)]
        u = [_dot(t.astype(BF16), r_.astype(BF16)) for t, r_ in zip(t_inv, rhs)]
        uv = [jnp.concatenate([u_.astype(BF16), vb], axis=0) for u_, vb in zip(u, vs_)]
        for i, hp in enumerate(hps):
            a_r = jnp.where(incl2, gram[i][P2:, :], 0.0).astype(BF16)
            o_bd = xst[i][P2:] + _dot(a_r, uv[i])
            o_ref[0, hp] = o_bd[:CHUNK] + o_bd[CHUNK:]
            s_sc[hp] = (s_old[i] + _dot_tn(uv[i], ys_[i])) * pre[i][3]


def _rw_scan(r, v, kk, lw, kd, bd, B, S):
    T = r.shape[1]
    nc = S // CHUNK + 1
    meta_chunk = (B * S + META_BLK - CHUNK) // CHUNK

    def blk(g, c):
        b = g // 2
        j = jnp.where(g % 2 == 0, c, nc - 1 - c)
        return jnp.where(j == 0, meta_chunk + b * (META_BLK // CHUNK), b * (S // CHUNK) + j - 1)

    shared = pl.BlockSpec((RW_PAIRS, CHUNK, LANE), lambda g, c: (0, blk(g, c), 0))
    per_dir = pl.BlockSpec((1, RW_PAIRS, CHUNK, LANE), lambda g, c: (g % 2, 0, blk(g, c), 0))
    return pl.pallas_call(
        _rw_scan_kernel,
        grid=(2 * B, nc),
        in_specs=[shared, shared, shared, per_dir, per_dir, per_dir],
        out_specs=per_dir,
        out_shape=jax.ShapeDtypeStruct((2, RW_PAIRS, T, LANE), F32),
        scratch_shapes=[pltpu.VMEM((RW_PAIRS, LANE, LANE), F32)],
        compiler_params=_cparams(("parallel", "arbitrary")),
        name="rwkv_scan",
    )(r, v, kk, lw, kd, bd)


def _rw_post_kernel(o_ref, r_ref, v_ref, kd_ref, g_ref, rk_ref, gw_ref, gb_ref, mean_ref, ones_ref, y_ref, *,
                    n_real_tiles):
    mean_m = mean_ref[...]
    ones = ones_ref[...]
    tl = y_ref.shape[0]
    rid = lax.broadcasted_iota(jnp.int32, (tl, 1), 0) % META_BLK
    live = rid >= jnp.where(pl.program_id(0) >= n_real_tiles, META_OFF, 0)
    for hp in range(RW_PAIRS):
        sl = slice(hp * LANE, (hp + 1) * LANE)
        y = o_ref[0, hp] + o_ref[1, hp]
        yc = y - _dot_split_lhs(y, mean_m)
        var = _dot_split_lhs(yc * yc, mean_m)
        yn = yc * lax.rsqrt(var + GN_EPS) * gw_ref[:, sl] + gb_ref[:, sl]
        k_bonus = 0.5 * (kd_ref[0, hp] + kd_ref[1, hp])
        bonus = _dot_split_lhs(r_ref[hp] * k_bonus * rk_ref[:, sl], ones) * v_ref[hp]
        y_ref[:, sl] = jnp.where(live, (yn + bonus) * g_ref[:, sl], 0.0).astype(y_ref.dtype)


def _rw_post(o, r, v, kd, g, prm, n_real):
    T = g.shape[0]
    tl = _row_tile(T, 256)
    assert n_real % tl == 0
    C = RW_WIDTH
    fix = lambda i: (0, 0)
    pair = pl.BlockSpec((RW_PAIRS, tl, LANE), lambda i: (0, i, 0))
    pair2 = pl.BlockSpec((2, RW_PAIRS, tl, LANE), lambda i: (0, 0, i, 0))
    return pl.pallas_call(
        functools.partial(_rw_post_kernel, n_real_tiles=n_real // tl),
        grid=(T // tl,),
        in_specs=[pair2, pair, pair, pair2,
                  pl.BlockSpec((tl, C), lambda i: (i, 0)),
                  pl.BlockSpec((1, C), fix), pl.BlockSpec((1, C), fix), pl.BlockSpec((1, C), fix),
                  pl.BlockSpec((LANE, LANE), fix), pl.BlockSpec((LANE, LANE), fix)],
        out_specs=pl.BlockSpec((tl, C), lambda i: (i, 0)),
        out_shape=jax.ShapeDtypeStruct((T, C), BF16),
        compiler_params=_cparams(("parallel",)),
        name="rwkv_post",
    )(o, r, v, kd, g, prm["r_k"], prm["gn_w"], prm["gn_b"],
      _pair_sum_matrix(1.0 / RW_HEAD), _pair_sum_matrix(1.0))


def _block_diag2(m):
    z = jnp.zeros_like(m[0])
    return jnp.concatenate([jnp.concatenate([m[0], z], axis=1), jnp.concatenate([z, m[1]], axis=1)], axis=0)


def _rwkv7(p_rw, mu_prev, mu_next, w0, w_up, a0, a_up, g_up, k_k, k_a, r_k, gn_w, gn_b, B, S):
    C = RW_WIDTH
    pad = RW_PROJ_PAD - RW_PROJ
    prm = {
        "mu_prev": jnp.pad(mu_prev, (0, pad)).reshape(1, RW_PROJ_PAD),
        "mu_next": jnp.pad(mu_next, (0, pad)).reshape(1, RW_PROJ_PAD),
        "w0": w0.reshape(1, 2 * C),
        "w_up": _block_diag2(w_up).astype(BF16),
        "a0": a0.reshape(1, 2 * C),
        "a_up": _block_diag2(a_up).astype(BF16),
        "g_up": jnp.pad(g_up, ((0, GATE_PAD - GATE_LORA), (0, 0))).astype(BF16),
        "k_k": k_k.reshape(1, C),
        "k_a": k_a.reshape(1, C),
        "r_k": r_k.reshape(1, C),
        "gn_w": gn_w.reshape(1, C),
        "gn_b": gn_b.reshape(1, C),
    }
    r, v, kk, lw, kd, bd, g = _rw_prep(p_rw, prm, B, S)
    o = _rw_scan(r, v, kk, lw, kd, bd, B, S)
    return _rw_post(o, r, v, kd, g, prm, B * S)


def kernel(x, meta_tokens, ffn1_w_in, ffn1_w_out, ln_ffn1_g, ln_ffn1_b, w_in, rw_mu_prev, rw_mu_next, rw_w0, rw_w_up, rw_a0, rw_a_up, rw_g_up, rw_k_k, rw_k_a, rw_r_k, rw_gn_w, rw_gn_b, na_rpb, gq_q_gain, gq_k_gain, w_out, ln_mix_g, ln_mix_b, ffn2_w_in, ffn2_w_out, ln_ffn2_g, ln_ffn2_b):
    B, S, D = x.shape
    assert D == D_MODEL and S % 1024 == 0
    meta = jnp.zeros((B, META_BLK, D), x.dtype).at[:, META_OFF:].set(
        jnp.broadcast_to(meta_tokens[None].astype(x.dtype), (B, N_META, D)))
    h = jnp.concatenate([x.reshape(B * S, D), meta.reshape(B * META_BLK, D)], axis=0)
    hb = h.astype(BF16)
    cos_t, sin_t = _rope_tables(B, S)
    for l in range(DEPTH):
        h, hb = _ffn(h, hb, ffn1_w_in[l], ffn1_w_out[l], ln_ffn1_g[l], ln_ffn1_b[l])
        w = w_in[l]
        w_rw = jnp.pad(w[:, :RW_PROJ], ((0, 0), (0, RW_PROJ_PAD - RW_PROJ))).astype(BF16)
        w_na = w[:, RW_PROJ:RW_PROJ + NA_PROJ].astype(BF16)
        w_gq = w[:, RW_PROJ + NA_PROJ:].astype(BF16)
        p_rw = _matmul(hb, w_rw, F32)
        p_na = _matmul(hb, w_na, BF16)
        p_gq = _matmul(hb, w_gq, F32)
        y_a = _rwkv7(p_rw, rw_mu_prev[l], rw_mu_next[l], rw_w0[l], rw_w_up[l], rw_a0[l], rw_a_up[l],
                     rw_g_up[l], rw_k_k[l], rw_k_a[l], rw_r_k[l], rw_gn_w[l], rw_gn_b[l], B, S)
        y_b = _neighborhood_attention(p_na, na_rpb[l], B, S)
        q, k, v = _gq_prep(p_gq, cos_t, sin_t, gq_q_gain[l], gq_k_gain[l])
        y_c = _gqa(q, k, v, B, S)
        z = _mix_out_matmul(y_a, y_b, y_c, w_out[l].astype(BF16), h)
        h, hb = _layer_norm(z, ln_mix_g[l], ln_mix_b[l])
        h, hb = _ffn(h, hb, ffn2_w_in[l], ffn2_w_out[l], ln_ffn2_g[l], ln_ffn2_b[l])
    return h[:B * S].reshape(B, S, D)
```
